```python
import jax
import jax.numpy as jnp
from jax import lax
import numpy as np

D_MODEL = 1024
BATCH = 4
SEQ = 4096
DEPTH = 4

GRID_W = 64
CTX_LEN = 256
D_MIX = D_MODEL
N_MIXERS = 4
W_GROUP = D_MIX // N_MIXERS
LRU_HEADS = 4
LRU_HD = W_GROUP // LRU_HEADS
LRU_C = 8.0
CONV_A_PAD = (2, 1)
POOL_WINDOWS = (2, 4, 8, 16)
POOL_HD = W_GROUP // len(POOL_WINDOWS)
CONV_C_PAD = (1, 1)
SGU_CHUNK = 128
SGU_HEADS = 4
SGU_HD = W_GROUP // SGU_HEADS
PROJ_SLABS = 8
PROJ_DIM = PROJ_SLABS * W_GROUP
PROJ_SPLITS = tuple(W_GROUP * k for k in range(1, PROJ_SLABS))
N_KEYS = 128
N_EXPERTS = N_KEYS * N_KEYS
PEER_HEADS = 8
PEER_QDIM = 256
PEER_HALF = PEER_QDIM // 2
PEER_TOPK = 16
PEER_BLOCK = 128
N_MOD = 6
EPS = 1e-6
POS_BASE = 10000.0

kernel_name = 'hybrid_lru_pool_conv_sgu_peer_dit'


def rmsnorm(x, g):
    xf = x.astype(jnp.float32)
    y = xf * lax.rsqrt(jnp.mean(xf * xf, axis=-1, keepdims=True) + EPS)
    return (y * g.astype(jnp.float32)).astype(x.dtype)


def layernorm(x, g):
    xf = x.astype(jnp.float32)
    xc = xf - jnp.mean(xf, axis=-1, keepdims=True)
    var = jnp.mean(xc * xc, axis=-1, keepdims=True)
    return (xc * lax.rsqrt(var + EPS) * g.astype(jnp.float32)).astype(x.dtype)


def modulate(x, shift, scale):
    return x * (1 + scale) + shift


def grid_sincos(n, dim, dtype):
    rows = n // GRID_W
    row = jnp.repeat(jnp.arange(rows, dtype=jnp.float32), GRID_W)
    col = jnp.tile(jnp.arange(GRID_W, dtype=jnp.float32), rows)
    quarter = dim // 4
    omega = POS_BASE ** (-jnp.arange(quarter, dtype=jnp.float32) / quarter)

    def axis_emb(p):
        ang = p[:, None] * omega[None, :]
        return jnp.concatenate([jnp.sin(ang), jnp.cos(ang)], axis=-1)

    return jnp.concatenate([axis_emb(row), axis_emb(col)], axis=-1).astype(dtype)


def dwconv(x, w, pad):
    return lax.conv_general_dilated(
        x, w[:, None, :].astype(x.dtype), window_strides=(1,), padding=[pad],
        dimension_numbers=('NWC', 'WIO', 'NWC'), feature_group_count=x.shape[-1])


def blockdiag(x, w, b):
    bsz, n, _ = x.shape
    xh = x.reshape(bsz, n, LRU_HEADS, LRU_HD)
    return (jnp.einsum('bnhi,hij->bnhj', xh, w) + b).reshape(bsz, n, LRU_HEADS * LRU_HD)


def linear_scan(a, b, h0):
    def combine(left, right):
        a1, b1 = left
        a2, b2 = right
        return a1 * a2, a2 * b1 + b2

    a_cum, b_cum = lax.associative_scan(combine, (a, b), axis=1)
    return a_cum * h0[:, None, :] + b_cum


def rglru_bidir(xa, lam, w_r, b_r, w_i, b_i, h0_f, h0_b):
    xf = xa.astype(jnp.float32)
    hs, finals = [], []
    for d, h0 in enumerate((h0_f, h0_b)):
        r = jax.nn.sigmoid(blockdiag(xa, w_r[d], b_r[d]).astype(jnp.float32))
        i = jax.nn.sigmoid(blockdiag(xa, w_i[d], b_i[d]).astype(jnp.float32))
        log_a = -LRU_C * r * jax.nn.softplus(-lam[d].astype(jnp.float32))
        a = jnp.exp(log_a)
        b = jnp.sqrt(-jnp.expm1(2.0 * log_a)) * (i * xf)
        if d == 1:
            a, b = jnp.flip(a, axis=1), jnp.flip(b, axis=1)
        h = linear_scan(a, b, h0)
        finals.append(h[:, -1])
        hs.append(jnp.flip(h, axis=1) if d == 1 else h)
    return hs[0] + hs[1], finals[0], finals[1]


def pool_mixer(x, w, scale):
    bsz, n, _ = x.shape
    xf = x.astype(jnp.float32)
    cs = jnp.pad(jnp.cumsum(xf, axis=1), ((0, 0), (1, 0), (0, 0)))
    t = jnp.arange(n)
    diffs = []
    for gi, win in enumerate(POOL_WINDOWS):
        lo = jnp.clip(t - win // 2, 0, n)
        hi = jnp.clip(t - win // 2 + win, 0, n)
        sl = slice(gi * POOL_HD, (gi + 1) * POOL_HD)
        cg = cs[:, :, sl]
        mean = (cg[:, hi] - cg[:, lo]) / (hi - lo).astype(jnp.float32)[None, :, None]
        diffs.append(mean - xf[:, :, sl])
    d = jnp.stack(diffs, axis=2).astype(x.dtype)
    y = jnp.einsum('bngi,gij->bngj', d, w).reshape(bsz, n, W_GROUP)
    return y * scale


def sgu(u, v, g, w_s, b_s):
    bsz, n, _ = u.shape
    u = jax.nn.gelu(u)
    v = layernorm(jax.nn.gelu(v), g)
    vc = v.reshape(bsz, n // SGU_CHUNK, SGU_CHUNK, SGU_HEADS, SGU_HD)
    mixed = jnp.einsum('bcphe,hqp->bcqhe', vc, w_s) + b_s.T[:, :, None]
    return u * mixed.reshape(bsz, n, W_GROUP)


def token_mix(p, lp, h0_f, h0_b):
    ax, ag, bx, cb, cc, ch, du, dv = jnp.split(p, PROJ_SPLITS, axis=-1)
    xa = dwconv(ax, lp['conv_a_w'], CONV_A_PAD) + lp['conv_a_b']
    h, fin_f, fin_b = rglru_bidir(xa, lp['lru_l'], lp['lru_wr'], lp['lru_br'],
                                  lp['lru_wi'], lp['lru_bi'], h0_f, h0_b)
    y_a = h.astype(ag.dtype) * jax.nn.gelu(ag)
    y_b = pool_mixer(bx, lp['pool_w'], lp['pool_scale'])
    y_c = cb * dwconv(cc * ch, lp['conv_c_w'], CONV_C_PAD)
    y_d = sgu(du, dv, lp['sgu_g'], lp['sgu_w'], lp['sgu_b'])
    return jnp.concatenate([y_a, y_b, y_c, y_d], axis=-1), fin_f, fin_b


def peer_ffn(tok, wq, k1, k2, u_tab, v_tab):
    n_tok, dim = tok.shape
    blocks = tok.reshape(n_tok // PEER_BLOCK, PEER_BLOCK, dim)

    def one_block(xb):
        q = (xb @ wq).reshape(PEER_BLOCK, PEER_HEADS, PEER_QDIM).astype(jnp.float32)
        s1 = jnp.einsum('thk,nk->thn', q[..., :PEER_HALF], k1.astype(jnp.float32))
        s2 = jnp.einsum('thk,nk->thn', q[..., PEER_HALF:], k2.astype(jnp.float32))
        v1, i1 = lax.top_k(s1, PEER_TOPK)
        v2, i2 = lax.top_k(s2, PEER_TOPK)
        cand = (v1[..., :, None] + v2[..., None, :]).reshape(PEER_BLOCK, PEER_HEADS, PEER_TOPK * PEER_TOPK)
        vs, ic = lax.top_k(cand, PEER_TOPK)
        e = (jnp.take_along_axis(i1, ic // PEER_TOPK, axis=-1) * N_KEYS
             + jnp.take_along_axis(i2, ic % PEER_TOPK, axis=-1))
        gate = jax.nn.softmax(vs, axis=-1)
        ue = u_tab[e]
        act = jax.nn.gelu(jnp.einsum('thkd,td->thk', ue, xb).astype(jnp.float32)) * gate
        ve = v_tab[e]
        return jnp.einsum('thk,thkd->td', act.astype(xb.dtype), ve)

    return lax.map(one_block, blocks).reshape(n_tok, dim)


def setup_inputs(seed: int = 0) -> dict:
    key = jax.random.key(seed)
    ks = jax.random.split(key, 32)

    def nrm(k, shape, std):
        return jax.random.normal(k, shape, jnp.float32) * std

    u = jax.random.uniform(ks[11], (DEPTH, 2, W_GROUP), jnp.float32, minval=0.9, maxval=0.999)
    s = u ** (1.0 / LRU_C)
    lru_l = jnp.log(s) - jnp.log1p(-s)
    return {
        'x': nrm(ks[0], (BATCH, SEQ, D_MODEL), 1.0),
        'c': nrm(ks[1], (BATCH, D_MODEL), 1.0),
        'ctx': nrm(ks[2], (BATCH, CTX_LEN, D_MODEL), 1.0),
        'c_ctx': nrm(ks[3], (D_MODEL,), 1.0),
        'w_ada': nrm(ks[4], (DEPTH, D_MODEL, N_MOD * D_MODEL), 0.5 * D_MODEL ** -0.5),
        'b_ada': nrm(ks[5], (DEPTH, N_MOD * D_MODEL), 0.01),
        'g_mix': 1.0 + nrm(ks[6], (DEPTH, D_MODEL), 0.02),
        'g_ffn': 1.0 + nrm(ks[7], (DEPTH, D_MODEL), 0.02),
        'w_in': nrm(ks[8], (DEPTH, D_MODEL, PROJ_DIM), D_MODEL ** -0.5),
        'w_out': nrm(ks[9], (DEPTH, D_MIX, D_MODEL), D_MIX ** -0.5),
        'conv_a_w': nrm(ks[10], (DEPTH, 4, W_GROUP), 0.5),
        'conv_a_b': nrm(ks[12], (DEPTH, W_GROUP), 0.01),
        'lru_l': lru_l,
        'lru_wr': nrm(ks[13], (DEPTH, 2, LRU_HEADS, LRU_HD, LRU_HD), LRU_HD ** -0.5),
        'lru_br': nrm(ks[14], (DEPTH, 2, LRU_HEADS, LRU_HD), 0.01),
        'lru_wi': nrm(ks[15], (DEPTH, 2, LRU_HEADS, LRU_HD, LRU_HD), LRU_HD ** -0.5),
        'lru_bi': nrm(ks[16], (DEPTH, 2, LRU_HEADS, LRU_HD), 0.01),
        'pool_w': nrm(ks[17], (DEPTH, len(POOL_WINDOWS), POOL_HD, POOL_HD), POOL_HD ** -0.5),
        'pool_scale': 1.0 + nrm(ks[18], (DEPTH, W_GROUP), 0.02),
        'conv_c_w': nrm(ks[19], (DEPTH, 3, W_GROUP), 3 ** -0.5),
        'sgu_g': 1.0 + nrm(ks[20], (DEPTH, W_GROUP), 0.02),
        'sgu_w': nrm(ks[21], (DEPTH, SGU_HEADS, SGU_CHUNK, SGU_CHUNK), SGU_CHUNK ** -0.5),
        'sgu_b': 1.0 + nrm(ks[22], (DEPTH, SGU_HEADS, SGU_CHUNK), 0.01),
        'peer_wq': nrm(ks[23], (DEPTH, D_MODEL, PEER_HEADS * PEER_QDIM), D_MODEL ** -0.5),
        'peer_k1': nrm(ks[24], (DEPTH, N_KEYS, PEER_HALF), PEER_HALF ** -0.5),
        'peer_k2': nrm(ks[25], (DEPTH, N_KEYS, PEER_HALF), PEER_HALF ** -0.5),
        'peer_u': nrm(ks[26], (DEPTH, N_EXPERTS, D_MODEL), D_MODEL ** -0.5),
        'peer_v': nrm(ks[27], (DEPTH, N_EXPERTS, D_MODEL), 1.0),
        'g_final': 1.0 + nrm(ks[28], (D_MODEL,), 0.02),
    }


def reference(x, c, ctx, c_ctx, w_ada, b_ada, g_mix, g_ffn, w_in, w_out, conv_a_w, conv_a_b,
              lru_l, lru_wr, lru_br, lru_wi, lru_bi, pool_w, pool_scale, conv_c_w,
              sgu_g, sgu_w, sgu_b, peer_wq, peer_k1, peer_k2, peer_u, peer_v, g_final):
    bsz, n, dim = x.shape
    pos = grid_sincos(n, dim, x.dtype)
    z_x, z_c = x, ctx
    h0 = jnp.zeros((ctx.shape[0], W_GROUP), jnp.float32)
    for l in range(DEPTH):
        last = l == DEPTH - 1
        lp = {'conv_a_w': conv_a_w[l], 'conv_a_b': conv_a_b[l], 'lru_l': lru_l[l],
              'lru_wr': lru_wr[l], 'lru_br': lru_br[l], 'lru_wi': lru_wi[l], 'lru_bi': lru_bi[l],
              'pool_w': pool_w[l], 'pool_scale': pool_scale[l], 'conv_c_w': conv_c_w[l],
              'sgu_g': sgu_g[l], 'sgu_w': sgu_w[l], 'sgu_b': sgu_b[l]}
        mod_x = jnp.split(jax.nn.silu(c) @ w_ada[l] + b_ada[l], N_MOD, axis=-1)
        mod_c = jnp.split(jax.nn.silu(c_ctx)[None] @ w_ada[l] + b_ada[l], N_MOD, axis=-1)
        sh1x, sc1x, g1x, sh2x, sc2x, g2x = [m[:, None, :] for m in mod_x]
        sh1c, sc1c, g1c, sh2c, sc2c, g2c = [m[:, None, :] for m in mod_c]

        a_x = modulate(rmsnorm(z_x, g_mix[l]), sh1x, sc1x) + pos
        a_c = modulate(rmsnorm(z_c, g_mix[l]), sh1c, sc1c)
        if last:
            xa_c = dwconv(a_c @ w_in[l][:, :W_GROUP], lp['conv_a_w'], CONV_A_PAD) + lp['conv_a_b']
            _, fin_f, fin_b = rglru_bidir(xa_c, lp['lru_l'], lp['lru_wr'], lp['lru_br'],
                                          lp['lru_wi'], lp['lru_bi'], h0, h0)
        else:
            y_c, fin_f, fin_b = token_mix(a_c @ w_in[l], lp, h0, h0)
            z_c = z_c + g1c * (y_c @ w_out[l])
        y_x, _, _ = token_mix(a_x @ w_in[l], lp, fin_f, fin_b)
        z_x = z_x + g1x * (y_x @ w_out[l])

        f_x = modulate(rmsnorm(z_x, g_ffn[l]), sh2x, sc2x)
        if last:
            out_x = peer_ffn(f_x.reshape(-1, dim), peer_wq[l], peer_k1[l], peer_k2[l], peer_u[l], peer_v[l])
            z_x = z_x + g2x * out_x.reshape(bsz, n, dim)
        else:
            f_c = modulate(rmsnorm(z_c, g_ffn[l]), sh2c, sc2c)
            n_c = z_c.shape[0] * z_c.shape[1]
            tok = jnp.concatenate([f_c.reshape(-1, dim), f_x.reshape(-1, dim)], axis=0)
            out = peer_ffn(tok, peer_wq[l], peer_k1[l], peer_k2[l], peer_u[l], peer_v[l])
            z_c = z_c + g2c * out[:n_c].reshape(z_c.shape)
            z_x = z_x + g2x * out[n_c:].reshape(bsz, n, dim)
    return rmsnorm(z_x, g_final)
```

```python
import functools

import jax
import jax.numpy as jnp
from jax import lax
from jax.experimental import pallas as pl
from jax.experimental.pallas import tpu as pltpu

F32 = jnp.float32
BF16 = jnp.bfloat16

W_GROUP = 256
LRU_HEADS = 4
LRU_C = 8.0
POOL_WINDOWS = (2, 4, 8, 16)
SGU_CHUNK = 128
SGU_HEADS = 4
PROJ_DIM = 8 * W_GROUP
N_KEYS = 128
PEER_HEADS = 8
PEER_HALF = 128
PEER_TOPK = 16
N_MOD = 6
EPS = 1e-6
POS_BASE = 10000.0
GRID_W = 64

TOK_TILE = 256
HALO = 8
PEER_TILE = 512
PEER_ECHUNK = 1024
PEER_LANES = 256
NEG_BIG = -3.0e38
VMEM_LIMIT = 56 * 1024 * 1024


def _gelu(x):
    return 0.5 * x * (1.0 + jnp.tanh(0.7978845608028654 * (x + 0.044715 * x * x * x)))


def _rms(x, g):
    return x * lax.rsqrt(jnp.mean(x * x, axis=-1, keepdims=True) + EPS) * g


def _ada_kernel(c_ref, w_ref, b_ref, o_ref):
    cv = c_ref[...]
    s = cv * jax.nn.sigmoid(cv)
    o_ref[...] = jnp.dot(s.astype(BF16), w_ref[...].astype(BF16),
                         preferred_element_type=F32) + b_ref[...]


def _ada_call(cvec, w_ada, b_ada):
    depth, d, nm = w_ada.shape
    tn = 1536
    return pl.pallas_call(
        _ada_kernel,
        grid=(depth, nm // tn),
        in_specs=[pl.BlockSpec((8, d), lambda l, j: (0, 0)),
                  pl.BlockSpec((None, d, tn), lambda l, j: (l, 0, j)),
                  pl.BlockSpec((None, 1, tn), lambda l, j: (l, 0, j))],
        out_specs=pl.BlockSpec((None, 8, tn), lambda l, j: (l, 0, j)),
        out_shape=jax.ShapeDtypeStruct((depth, 8, nm), F32),
        compiler_params=pltpu.CompilerParams(vmem_limit_bytes=VMEM_LIMIT),
        name="ada",
    )(cvec, w_ada, b_ada.reshape(depth, 1, nm))


def _mix_in_kernel(nct, tps, zp_ref, z_ref, zn_ref, pp_ref, p_ref, pn_ref, mod_ref, gmix_ref,
                   win_ref, cwa_ref, cba_ref, cwc_ref, poolw_ref, pscale_ref, sgug_ref,
                   sguw_ref, sgub_ref, xa_ref, gag_ref, ybcd_ref,
                   ext_scr, sa_scr, sb_scr, sc_scr, sd_scr):
    tt = TOK_TILE
    i = pl.program_id(0)
    is_ctx = i < nct
    pos_tile = jnp.where(is_ctx, 0, (i - nct) % tps)
    seq_tiles = jnp.where(is_ctx, 1, tps)
    first = pos_tile == 0
    last = pos_tile == seq_tiles - 1
    t0 = pos_tile * tt
    n_seq = seq_tiles * tt

    z_ext = jnp.concatenate([zp_ref[...], z_ref[...], zn_ref[...]], axis=0)
    pos_ext = jnp.concatenate([pp_ref[...], p_ref[...], pn_ref[...]], axis=0)
    a = _rms(z_ext, gmix_ref[...])
    a = a * (1.0 + mod_ref[1:2, :]) + mod_ref[0:1, :] + pos_ext
    row = lax.broadcasted_iota(jnp.int32, (tt + 2 * HALO, 1), 0)
    keep_prev = jnp.where(first, 0.0, 1.0)
    keep_next = jnp.where(last, 0.0, 1.0)
    keep = jnp.where(row < HALO, keep_prev, jnp.where(row >= tt + HALO, keep_next, 1.0))
    a = a * keep
    p = jnp.dot(a.astype(BF16), win_ref[...], preferred_element_type=F32)

    zeros8 = jnp.zeros((HALO, PROJ_DIM), F32)
    ext_scr[0:HALO, :] = zeros8
    ext_scr[tt + 3 * HALO:tt + 4 * HALO, :] = zeros8
    ext_scr[HALO:tt + 3 * HALO, :] = p
    m0 = 2 * HALO
    ne = tt + 2 * HALO

    cwa = cwa_ref[...]
    xa = cba_ref[...]
    for k in range(4):
        xa = xa + cwa[k:k + 1, :] * ext_scr[pl.ds(m0 - 2 + k, tt), 0:W_GROUP]
    xa_ref[...] = xa
    gag_ref[...] = _gelu(ext_scr[pl.ds(m0, tt), W_GROUP:2 * W_GROUP])

    z8 = jnp.zeros((HALO, W_GROUP), F32)
    for scr in (sa_scr, sb_scr, sc_scr, sd_scr):
        scr[0:HALO, :] = z8
        scr[tt + 3 * HALO:tt + 4 * HALO, :] = z8
    c0 = 2 * W_GROUP
    sa_scr[HALO:HALO + ne, :] = (ext_scr[pl.ds(HALO - 1, ne), c0:c0 + W_GROUP]
                                 + ext_scr[pl.ds(HALO, ne), c0:c0 + W_GROUP])
    sb_scr[HALO:HALO + ne, :] = sa_scr[pl.ds(HALO - 1, ne), :] + sa_scr[pl.ds(HALO + 1, ne), :]
    sc_scr[HALO:HALO + ne, :] = sb_scr[pl.ds(HALO - 2, ne), :] + sb_scr[pl.ds(HALO + 2, ne), :]
    p16 = sc_scr[pl.ds(m0 - 4, tt), :] + sc_scr[pl.ds(m0 + 4, tt), :]
    p2 = sa_scr[pl.ds(m0, tt), :]
    p4 = sb_scr[pl.ds(m0, tt), :]
    p8 = sc_scr[pl.ds(m0, tt), :]
    grp = lax.broadcasted_iota(jnp.int32, (tt, W_GROUP), 1) // (W_GROUP // len(POOL_WINDOWS))
    tpos = t0 + lax.broadcasted_iota(jnp.int32, (tt, W_GROUP), 0)
    sums = jnp.where(grp == 0, p2, jnp.where(grp == 1, p4, jnp.where(grp == 2, p8, p16)))
    half = jnp.where(grp == 0, 1, jnp.where(grp == 1, 2, jnp.where(grp == 2, 4, 8)))
    cnt = jnp.minimum(tpos + half, n_seq) - jnp.maximum(tpos - half, 0)
    bx = ext_scr[pl.ds(m0, tt), c0:c0 + W_GROUP]
    dpool = sums / cnt.astype(F32) - bx
    y_b = jnp.dot(dpool.astype(BF16), poolw_ref[...], preferred_element_type=F32) * pscale_ref[...]
    ybcd_ref[:, 0:W_GROUP] = y_b.astype(BF16)

    c_cb, c_cc, c_ch = 3 * W_GROUP, 4 * W_GROUP, 5 * W_GROUP
    sd_scr[HALO:HALO + ne, :] = (ext_scr[pl.ds(HALO, ne), c_cc:c_cc + W_GROUP]
                                 * ext_scr[pl.ds(HALO, ne), c_ch:c_ch + W_GROUP])
    cwc = cwc_ref[...]
    conv = jnp.zeros((tt, W_GROUP), F32)
    for k in range(3):
        conv = conv + cwc[k:k + 1, :] * sd_scr[pl.ds(m0 - 1 + k, tt), :]
    y_c = ext_scr[pl.ds(m0, tt), c_cb:c_cb + W_GROUP] * conv
    ybcd_ref[:, W_GROUP:2 * W_GROUP] = y_c.astype(BF16)

    c_du, c_dv = 6 * W_GROUP, 7 * W_GROUP
    u = _gelu(ext_scr[pl.ds(m0, tt), c_du:c_du + W_GROUP])
    gv = _gelu(ext_scr[pl.ds(m0, tt), c_dv:c_dv + W_GROUP])
    xc = gv - jnp.mean(gv, axis=-1, keepdims=True)
    v = xc * lax.rsqrt(jnp.mean(xc * xc, axis=-1, keepdims=True) + EPS) * sgug_ref[...]
    head = lax.broadcasted_iota(jnp.int32, (SGU_CHUNK, W_GROUP), 1) // (W_GROUP // SGU_HEADS)
    for c in range(tt // SGU_CHUNK):
        vc = v[c * SGU_CHUNK:(c + 1) * SGU_CHUNK, :]
        stack = jnp.concatenate([jnp.where(head == h, vc, 0.0) for h in range(SGU_HEADS)], axis=0)
        mixed = jnp.dot(sguw_ref[...], stack.astype(BF16), preferred_element_type=F32) + sgub_ref[...]
        y_d = u[c * SGU_CHUNK:(c + 1) * SGU_CHUNK, :] * mixed
        ybcd_ref[c * SGU_CHUNK:(c + 1) * SGU_CHUNK, 2 * W_GROUP:3 * W_GROUP] = y_d.astype(BF16)


def _mix_in_call(z, pos_tab, mods, lw, nct, tps, n_batch):
    n, d = z.shape
    tt = TOK_TILE
    nt = n // tt
    hb = tt // HALO
    n_pos = pos_tab.shape[0]

    def mrow(i):
        return jnp.where(i < nct, n_batch, (i - nct) // tps)

    def pblk(i):
        return jnp.where(i < nct, 0, 1 + (i - nct) % tps)

    full = lambda shape: pl.BlockSpec(shape, lambda i: (0,) * len(shape))
    in_specs = [
        pl.BlockSpec((HALO, d), lambda i: (jnp.maximum(i * hb - 1, 0), 0)),
        pl.BlockSpec((tt, d), lambda i: (i, 0)),
        pl.BlockSpec((HALO, d), lambda i: (jnp.minimum((i + 1) * hb, n // HALO - 1), 0)),
        pl.BlockSpec((HALO, d), lambda i: (jnp.maximum(pblk(i) * hb - 1, 0), 0)),
        pl.BlockSpec((tt, d), lambda i: (pblk(i), 0)),
        pl.BlockSpec((HALO, d), lambda i: (jnp.minimum((pblk(i) + 1) * hb, n_pos // HALO - 1), 0)),
        pl.BlockSpec((None, N_MOD, d), lambda i: (mrow(i), 0, 0)),
        full((1, d)),
        full((d, PROJ_DIM)),
        full((4, W_GROUP)), full((1, W_GROUP)), full((3, W_GROUP)),
        full((W_GROUP, W_GROUP)), full((1, W_GROUP)), full((1, W_GROUP)),
        full((SGU_CHUNK, SGU_HEADS * SGU_CHUNK)), full((SGU_CHUNK, W_GROUP)),
    ]
    out_specs = [pl.BlockSpec((tt, W_GROUP), lambda i: (i, 0)),
                 pl.BlockSpec((tt, W_GROUP), lambda i: (i, 0)),
                 pl.BlockSpec((tt, 3 * W_GROUP), lambda i: (i, 0))]
    out_shape = [jax.ShapeDtypeStruct((n, W_GROUP), F32),
                 jax.ShapeDtypeStruct((n, W_GROUP), F32),
                 jax.ShapeDtypeStruct((n, 3 * W_GROUP), BF16)]
    er = tt + 4 * HALO
    return pl.pallas_call(
        functools.partial(_mix_in_kernel, nct, tps),
        grid=(nt,),
        in_specs=in_specs, out_specs=out_specs, out_shape=out_shape,
        scratch_shapes=[pltpu.VMEM((er, PROJ_DIM), F32)] + [pltpu.VMEM((er, W_GROUP), F32)] * 4,
        compiler_params=pltpu.CompilerParams(vmem_limit_bytes=VMEM_LIMIT),
        name="mix_in",
    )(z, z, z, pos_tab, pos_tab, pos_tab, mods, lw["g_mix"], lw["w_in"], lw["conv_a_w"],
      lw["conv_a_b"], lw["conv_c_w"], lw["pool_w"], lw["pool_scale"], lw["sgu_g"],
      lw["sgu_w"], lw["sgu_b"])


def _chunk_scan(a, b, reverse):
    t = a.shape[0]
    row = lax.broadcasted_iota(jnp.int32, a.shape, 0)
    d = 1
    while d < t:
        shift = t - d if reverse else d
        ok = (row < t - d) if reverse else (row >= d)
        a_s = jnp.where(ok, pltpu.roll(a, shift, axis=0), 1.0)
        b_s = jnp.where(ok, pltpu.roll(b, shift, axis=0), 0.0)
        b = b + a * b_s
        a = a * a_s
        d *= 2
    return a, b


def _scan_kernel(xf_ref, xb_ref, wg_ref, bg_ref, lam_ref, hf_ref, hb_ref, carry_ref):
    tt = TOK_TILE
    s = pl.program_id(1)

    @pl.when(s == 0)
    def _():
        carry_ref[...] = jnp.zeros_like(carry_ref)

    for d, (x_ref, o_ref) in enumerate(((xf_ref, hf_ref), (xb_ref, hb_ref))):
        xa = x_ref[...]
        g = jnp.dot(xa.astype(BF16), wg_ref[d], preferred_element_type=F32) + bg_ref[d]
        r = jax.nn.sigmoid(g[:, 0:W_GROUP])
        gi = jax.nn.sigmoid(g[:, W_GROUP:2 * W_GROUP])
        neg_lam = -lam_ref[d:d + 1, :]
        softplus = jnp.maximum(neg_lam, 0.0) + jnp.log1p(jnp.exp(-jnp.abs(neg_lam)))
        log_a = (-LRU_C) * r * softplus
        a = jnp.exp(log_a)
        b = jnp.sqrt(1.0 - a * a) * (gi * xa)
        a_cum, h0 = _chunk_scan(a, b, reverse=(d == 1))
        h = h0 + a_cum * carry_ref[d:d + 1, :]
        o_ref[...] = h
        edge = 0 if d == 1 else tt - 1
        carry_ref[d:d + 1, :] = h[edge:edge + 1, :]


def _scan_call(xa, lw, nct, tps, n_batch):
    n, c = xa.shape
    tt = TOK_TILE
    fwd = lambda b, s: (jnp.where(s == 0, b, nct + b * tps + s - 1), 0)
    bwd = lambda b, s: (jnp.where(s == 0, b, nct + b * tps + tps - s), 0)
    full = lambda shape: pl.BlockSpec(shape, lambda b, s: (0,) * len(shape))
    return pl.pallas_call(
        _scan_kernel,
        grid=(n_batch, tps + 1),
        in_specs=[pl.BlockSpec((tt, c), fwd), pl.BlockSpec((tt, c), bwd),
                  full((2, c, 2 * c)), full((2, 1, 2 * c)), full((2, c))],
        out_specs=[pl.BlockSpec((tt, c), fwd), pl.BlockSpec((tt, c), bwd)],
        out_shape=[jax.ShapeDtypeStruct((n, c), F32)] * 2,
        scratch_shapes=[pltpu.VMEM((2, c), F32)],
        compiler_params=pltpu.CompilerParams(vmem_limit_bytes=VMEM_LIMIT),
        name="lru_scan",
    )(xa, xa, lw["lru_wg"], lw["lru_bg"], lw["lru_l"])


def _top16(cur, store):
    for r in range(PEER_TOPK):
        mx = jnp.max(cur, axis=0, keepdims=True)
        store(r, mx)
        cur = jnp.where(cur == mx, NEG_BIG, cur)


def _mid_kernel(z_ref, hf_ref, hb_ref, gag_ref, ybcd_ref, mod_ref, wout_ref, gffn_ref, wqt_ref,
                k1_ref, k2_ref, zo_ref, ft_ref, s1_ref, s2_ref, e1_ref, e2_ref, tau_ref,
                qt_scr, v1_scr, v2_scr):
    ya = ((hf_ref[...] + hb_ref[...]) * gag_ref[...]).astype(BF16)
    y = jnp.concatenate([ya, ybcd_ref[...]], axis=1)
    o = jnp.dot(y, wout_ref[...], preferred_element_type=F32)
    z = z_ref[...] + mod_ref[2:3, :] * o
    zo_ref[...] = z
    f = _rms(z, gffn_ref[...]) * (1.0 + mod_ref[4:5, :]) + mod_ref[3:4, :]
    ft = f.T.astype(BF16)
    ft_ref[...] = ft
    qt_scr[...] = jnp.dot(wqt_ref[...], ft, preferred_element_type=F32)

    def head_body(h, carry):
        base = pl.multiple_of(h * (2 * PEER_HALF), 2 * PEER_HALF)
        q1 = qt_scr[pl.ds(base, PEER_HALF), :].astype(BF16)
        q2 = qt_scr[pl.ds(base + PEER_HALF, PEER_HALF), :].astype(BF16)
        s1 = jnp.dot(k1_ref[...], q1, preferred_element_type=F32)
        s2 = jnp.dot(k2_ref[...], q2, preferred_element_type=F32)
        s1_ref[h] = s1
        s2_ref[h] = s2

        def st1(r, row):
            v1_scr[r:r + 1, :] = row

        def st2(r, row):
            v2_scr[r:r + 1, :] = row

        _top16(s1, st1)
        _top16(s2, st2)
        v1 = v1_scr[...]
        v2 = v2_scr[...]
        blocks = [v2 + v1[0:1, :]]
        for a in range(1, 8):
            blocks.append(v2[0:8, :] + v1[a:a + 1, :])
        blocks.append(v1[8:16, :] + v2[0:1, :])
        cand = jnp.concatenate(blocks, axis=0)
        tops = []
        _top16(cand, lambda r, row: tops.append(row))
        m = tops[0]
        zsum = jnp.zeros_like(m)
        for row in tops:
            zsum = zsum + jnp.exp(row - m)
        tau_ref[pl.ds(h, 1), :] = tops[PEER_TOPK - 1]
        e1_ref[h] = jnp.exp(s1 - v1[0:1, :]) * (1.0 / zsum)
        e2_ref[h] = jnp.exp(s2 - v2[0:1, :])
        return carry

    lax.fori_loop(0, PEER_HEADS, head_body, 0)


def _mid_call(z, hf, hb, gag, ybcd, mods, lw, nct, tps, n_batch):
    n, d = z.shape
    tt = TOK_TILE
    nt = n // tt
    nq = PEER_HEADS * 2 * PEER_HALF

    def mrow(i):
        return jnp.where(i < nct, n_batch, (i - nct) // tps)

    full = lambda shape: pl.BlockSpec(shape, lambda i: (0,) * len(shape))
    tok = lambda w: pl.BlockSpec((tt, w), lambda i: (i, 0))
    in_specs = [tok(d), tok(W_GROUP), tok(W_GROUP), tok(W_GROUP), tok(3 * W_GROUP),
                pl.BlockSpec((None, N_MOD, d), lambda i: (mrow(i), 0, 0)),
                full((d, d)), full((1, d)), full((nq, d)),
                full((N_KEYS, PEER_HALF)), full((N_KEYS, PEER_HALF))]
    hk = lambda: pl.BlockSpec((PEER_HEADS, N_KEYS, tt), lambda i: (0, 0, i))
    out_specs = [tok(d), pl.BlockSpec((d, tt), lambda i: (0, i)), hk(), hk(), hk(), hk(),
                 pl.BlockSpec((PEER_HEADS, tt), lambda i: (0, i))]
    hk_shape = jax.ShapeDtypeStruct((PEER_HEADS, N_KEYS, n), F32)
    out_shape = [jax.ShapeDtypeStruct((n, d), F32), jax.ShapeDtypeStruct((d, n), BF16),
                 hk_shape, hk_shape, hk_shape, hk_shape,
                 jax.ShapeDtypeStruct((PEER_HEADS, n), F32)]
    return pl.pallas_call(
        _mid_kernel,
        grid=(nt,),
        in_specs=in_specs, out_specs=out_specs, out_shape=out_shape,
        scratch_shapes=[pltpu.VMEM((nq, tt), F32), pltpu.VMEM((PEER_TOPK, tt), F32),
                        pltpu.VMEM((PEER_TOPK, tt), F32)],
        compiler_params=pltpu.CompilerParams(vmem_limit_bytes=VMEM_LIMIT),
        name="mid",
    )(z, hf, hb, gag, ybcd, mods, lw["w_out"], lw["g_ffn"], lw["wq_t"], lw["k1"], lw["k2"])


def _peer_kernel(ft_ref, s1_ref, e1_ref, s2_ref, e2_ref, tau_ref, u_ref, vt_ref, z_ref, mod_ref,
                 zo_ref, acc_ref, h_scr, a_scr):
    c = pl.program_id(1)
    tl = PEER_LANES
    n_lt = PEER_TILE // tl
    rows_per_chunk = PEER_ECHUNK // N_KEYS

    @pl.when(c == 0)
    def _():
        acc_ref[...] = jnp.zeros_like(acc_ref)

    h_scr[...] = jnp.dot(u_ref[...], ft_ref[...], preferred_element_type=F32)

    def block(idx, carry):
        il = idx // n_lt
        lt = idx % n_lt
        r0 = pl.multiple_of(il * N_KEYS, N_KEYS)
        l0 = pl.multiple_of(lt * tl, tl)
        w = jnp.zeros((N_KEYS, tl), F32)
        for h in range(PEER_HEADS):
            s1r = s1_ref[h, pl.ds(il, 1), pl.ds(l0, tl)]
            e1r = e1_ref[h, pl.ds(il, 1), pl.ds(l0, tl)]
            s2 = s2_ref[h, :, pl.ds(l0, tl)]
            e2 = e2_ref[h, :, pl.ds(l0, tl)]
            tau = tau_ref[h:h + 1, pl.ds(l0, tl)]
            w = w + jnp.where(s1r + s2 >= tau, e1r * e2, 0.0)
        act = _gelu(h_scr[pl.ds(r0, N_KEYS), pl.ds(l0, tl)]) * w
        a_scr[pl.ds(r0, N_KEYS), pl.ds(l0, tl)] = act.astype(BF16)
        return carry

    lax.fori_loop(0, rows_per_chunk * n_lt, block, 0)
    acc_ref[...] += jnp.dot(vt_ref[...], a_scr[...], preferred_element_type=F32)

    @pl.when(c == pl.num_programs(1) - 1)
    def _():
        zo_ref[...] = z_ref[...] + mod_ref[5:6, :] * acc_ref[...].T


def _peer_call(z, ft, s1, s2, e1, e2, tau, mods, lw, n_ctx_tok, seq, n_batch):
    n, d = z.shape
    t = PEER_TILE
    ec = PEER_ECHUNK
    n_exp = lw["u"].shape[0]
    nct = n_ctx_tok // t
    tps = seq // t
    ic = ec // N_KEYS

    def mrow(i):
        return jnp.where(i < nct, n_batch, (i - nct) // tps)

    in_specs = [
        pl.BlockSpec((d, t), lambda i, c: (0, i)),
        pl.BlockSpec((PEER_HEADS, ic, t), lambda i, c: (0, c, i)),
        pl.BlockSpec((PEER_HEADS, ic, t), lambda i, c: (0, c, i)),
        pl.BlockSpec((PEER_HEADS, N_KEYS, t), lambda i, c: (0, 0, i)),
        pl.BlockSpec((PEER_HEADS, N_KEYS, t), lambda i, c: (0, 0, i)),
        pl.BlockSpec((PEER_HEADS, t), lambda i, c: (0, i)),
        pl.BlockSpec((ec, d), lambda i, c: (c, 0)),
        pl.BlockSpec((d, ec), lambda i, c: (0, c)),
        pl.BlockSpec((t, d), lambda i, c: (i, 0)),
        pl.BlockSpec((None, N_MOD, d), lambda i, c: (mrow(i), 0, 0)),
    ]
    return pl.pallas_call(
        _peer_kernel,
        grid=(n // t, n_exp // ec),
        in_specs=in_specs,
        out_specs=pl.BlockSpec((t, d), lambda i, c: (i, 0)),
        out_shape=jax.ShapeDtypeStruct((n, d), F32),
        scratch_shapes=[pltpu.VMEM((d, t), F32), pltpu.VMEM((ec, t), F32), pltpu.VMEM((ec, t), BF16)],
        compiler_params=pltpu.CompilerParams(vmem_limit_bytes=VMEM_LIMIT),
        name="peer",
    )(ft, s1, e1, s2, e2, tau, lw["u"], lw["v_t"], z, mods)


def _final_kernel(z_ref, g_ref, o_ref):
    o_ref[...] = _rms(z_ref[...], g_ref[...])


def _final_call(z, g_final, n_ctx_tok):
    n, d = z.shape
    tt = TOK_TILE
    off = n_ctx_tok // tt
    return pl.pallas_call(
        _final_kernel,
        grid=((n - n_ctx_tok) // tt,),
        in_specs=[pl.BlockSpec((tt, d), lambda i: (i + off, 0)), pl.BlockSpec((1, d), lambda i: (0, 0))],
        out_specs=pl.BlockSpec((tt, d), lambda i: (i, 0)),
        out_shape=jax.ShapeDtypeStruct((n - n_ctx_tok, d), F32),
        name="final_norm",
    )(z, g_final.reshape(1, d))


def _block_diag(w):
    h, a, b = w.shape
    eye = jnp.eye(h, dtype=w.dtype)
    return (eye[:, None, :, None] * w[:, :, None, :]).reshape(h * a, h * b)


def _grid_sincos(n, dim):
    rows = n // GRID_W
    row = jnp.repeat(jnp.arange(rows, dtype=F32), GRID_W)
    col = jnp.tile(jnp.arange(GRID_W, dtype=F32), rows)
    quarter = dim // 4
    omega = POS_BASE ** (-jnp.arange(quarter, dtype=F32) / quarter)

    def axis_emb(p):
        ang = p[:, None] * omega[None, :]
        return jnp.concatenate([jnp.sin(ang), jnp.cos(ang)], axis=-1)

    return jnp.concatenate([axis_emb(row), axis_emb(col)], axis=-1)


def kernel(x, c, ctx, c_ctx, w_ada, b_ada, g_mix, g_ffn, w_in, w_out, conv_a_w, conv_a_b, lru_l, lru_wr, lru_br, lru_wi, lru_bi, pool_w, pool_scale, conv_c_w, sgu_g, sgu_w, sgu_b, peer_wq, peer_k1, peer_k2, peer_u, peer_v, g_final):
    bsz, seq, d = x.shape
    ctx_len = ctx.shape[1]
    depth = w_ada.shape[0]
    assert ctx_len == TOK_TILE and seq % TOK_TILE == 0 and seq % PEER_TILE == 0
    assert (bsz * ctx_len) % PEER_TILE == 0 and bsz + 1 <= 8
    n_ctx_tok = bsz * ctx_len
    nct = n_ctx_tok // TOK_TILE
    tps = seq // TOK_TILE

    cvec = jnp.concatenate([c, c_ctx[None, :], jnp.zeros((8 - bsz - 1, d), F32)], axis=0)
    mods_all = _ada_call(cvec, w_ada, b_ada).reshape(depth, 8, N_MOD, d)

    pos_tab = jnp.concatenate([jnp.zeros((TOK_TILE, d), F32), _grid_sincos(seq, d)], axis=0)
    z = jnp.concatenate([ctx.reshape(n_ctx_tok, d), x.reshape(bsz * seq, d)], axis=0)

    for l in range(depth):
        wg = jnp.stack([jnp.concatenate([_block_diag(lru_wr[l, dd]), _block_diag(lru_wi[l, dd])], axis=1)
                        for dd in range(2)]).astype(BF16)
        bg = jnp.stack([jnp.concatenate([lru_br[l, dd].reshape(1, -1), lru_bi[l, dd].reshape(1, -1)], axis=1)
                        for dd in range(2)])
        lw = {
            "g_mix": g_mix[l].reshape(1, d), "g_ffn": g_ffn[l].reshape(1, d),
            "w_in": w_in[l].astype(BF16), "w_out": w_out[l].astype(BF16),
            "conv_a_w": conv_a_w[l], "conv_a_b": conv_a_b[l].reshape(1, -1),
            "conv_c_w": conv_c_w[l],
            "pool_w": _block_diag(pool_w[l]).astype(BF16), "pool_scale": pool_scale[l].reshape(1, -1),
            "sgu_g": sgu_g[l].reshape(1, -1),
            "sgu_w": jnp.transpose(sgu_w[l], (1, 0, 2)).reshape(SGU_CHUNK, SGU_HEADS * SGU_CHUNK).astype(BF16),
            "sgu_b": jnp.repeat(sgu_b[l].T, W_GROUP // SGU_HEADS, axis=1),
            "lru_wg": wg, "lru_bg": bg, "lru_l": lru_l[l],
            "wq_t": peer_wq[l].T.astype(BF16), "k1": peer_k1[l].astype(BF16), "k2": peer_k2[l].astype(BF16),
            "u": peer_u[l].astype(BF16), "v_t": peer_v[l].T.astype(BF16),
        }
        mods = mods_all[l]
        xa, gag, ybcd = _mix_in_call(z, pos_tab, mods, lw, nct, tps, bsz)
        hf, hb = _scan_call(xa, lw, nct, tps, bsz)
        z, ft, s1, s2, e1, e2, tau = _mid_call(z, hf, hb, gag, ybcd, mods, lw, nct, tps, bsz)
        z = _peer_call(z, ft, s1, s2, e1, e2, tau, mods, lw, n_ctx_tok, seq, bsz)

    out = _final_call(z, g_final, n_ctx_tok)
    return out.reshape(bsz, seq, d)
```

```python
import functools

import jax
import jax.numpy as jnp
from jax import lax
from jax.experimental import pallas as pl
from jax.experimental.pallas import tpu as pltpu

F32 = jnp.float32
BF16 = jnp.bfloat16

W_GROUP = 256
LRU_HEADS = 4
LRU_C = 8.0
POOL_WINDOWS = (2, 4, 8, 16)
SGU_CHUNK = 128
SGU_HEADS = 4
PROJ_DIM = 8 * W_GROUP
N_KEYS = 128
PEER_HEADS = 8
PEER_HALF = 128
PEER_TOPK = 16
N_MOD = 6
EPS = 1e-6
POS_BASE = 10000.0
GRID_W = 64

TOK_TILE = 256
HALO = 8
PEER_TILE = 512
PEER_ECHUNK = 2048
PEER_SUB = 512
PEER_LANES = 256
BF16_ROWS = 16
NEG_BIG = -3.0e38
VMEM_LIMIT = 56 * 1024 * 1024


def _gelu(x):
    return 0.5 * x * (1.0 + jnp.tanh(0.7978845608028654 * (x + 0.044715 * x * x * x)))


def _rms(x, g):
    return x * lax.rsqrt(jnp.mean(x * x, axis=-1, keepdims=True) + EPS) * g


def _ada_kernel(c_ref, w_ref, b_ref, o_ref):
    cv = c_ref[...]
    s = cv * jax.nn.sigmoid(cv)
    o_ref[...] = jnp.dot(s.astype(BF16), w_ref[...].astype(BF16),
                         preferred_element_type=F32) + b_ref[...]


def _ada_call(cvec, w_ada, b_ada):
    depth, d, nm = w_ada.shape
    tn = 1536
    return pl.pallas_call(
        _ada_kernel,
        grid=(depth, nm // tn),
        in_specs=[pl.BlockSpec((8, d), lambda l, j: (0, 0)),
                  pl.BlockSpec((None, d, tn), lambda l, j: (l, 0, j)),
                  pl.BlockSpec((None, 1, tn), lambda l, j: (l, 0, j))],
        out_specs=pl.BlockSpec((None, 8, tn), lambda l, j: (l, 0, j)),
        out_shape=jax.ShapeDtypeStruct((depth, 8, nm), F32),
        compiler_params=pltpu.CompilerParams(vmem_limit_bytes=VMEM_LIMIT),
        name="ada",
    )(cvec, w_ada, b_ada.reshape(depth, 1, nm))


def _mix_in_kernel(nct, tps, zp_ref, z_ref, zn_ref, pp_ref, p_ref, pn_ref, mod_ref, gmix_ref,
                   win_ref, cwa_ref, cba_ref, cwc_ref, poolw_ref, pscale_ref, sgug_ref,
                   sguw_ref, sgub_ref, xa_ref, gag_ref, ybcd_ref,
                   ext_scr, sa_scr, sb_scr, sc_scr, sd_scr):
    tt = TOK_TILE
    i = pl.program_id(0)
    is_ctx = i < nct
    pos_tile = jnp.where(is_ctx, 0, (i - nct) % tps)
    seq_tiles = jnp.where(is_ctx, 1, tps)
    first = pos_tile == 0
    last = pos_tile == seq_tiles - 1
    t0 = pos_tile * tt
    n_seq = seq_tiles * tt

    z_ext = jnp.concatenate([zp_ref[...], z_ref[...], zn_ref[...]], axis=0)
    pos_ext = jnp.concatenate([pp_ref[...], p_ref[...], pn_ref[...]], axis=0)
    a = _rms(z_ext, gmix_ref[...])
    a = a * (1.0 + mod_ref[1:2, :]) + mod_ref[0:1, :] + pos_ext
    row = lax.broadcasted_iota(jnp.int32, (tt + 2 * HALO, 1), 0)
    keep_prev = jnp.where(first, 0.0, 1.0)
    keep_next = jnp.where(last, 0.0, 1.0)
    keep = jnp.where(row < HALO, keep_prev, jnp.where(row >= tt + HALO, keep_next, 1.0))
    a = a * keep
    p = jnp.dot(a.astype(BF16), win_ref[...], preferred_element_type=F32)

    zeros8 = jnp.zeros((HALO, PROJ_DIM), F32)
    ext_scr[0:HALO, :] = zeros8
    ext_scr[tt + 3 * HALO:tt + 4 * HALO, :] = zeros8
    ext_scr[HALO:tt + 3 * HALO, :] = p
    m0 = 2 * HALO
    ne = tt + 2 * HALO

    cwa = cwa_ref[...]
    xa = cba_ref[...]
    for k in range(4):
        xa = xa + cwa[k:k + 1, :] * ext_scr[pl.ds(m0 - 2 + k, tt), 0:W_GROUP]
    xa_ref[...] = xa
    gag_ref[...] = _gelu(ext_scr[pl.ds(m0, tt), W_GROUP:2 * W_GROUP])

    z8 = jnp.zeros((HALO, W_GROUP), F32)
    for scr in (sa_scr, sb_scr, sc_scr, sd_scr):
        scr[0:HALO, :] = z8
        scr[tt + 3 * HALO:tt + 4 * HALO, :] = z8
    c0 = 2 * W_GROUP
    sa_scr[HALO:HALO + ne, :] = (ext_scr[pl.ds(HALO - 1, ne), c0:c0 + W_GROUP]
                                 + ext_scr[pl.ds(HALO, ne), c0:c0 + W_GROUP])
    sb_scr[HALO:HALO + ne, :] = sa_scr[pl.ds(HALO - 1, ne), :] + sa_scr[pl.ds(HALO + 1, ne), :]
    sc_scr[HALO:HALO + ne, :] = sb_scr[pl.ds(HALO - 2, ne), :] + sb_scr[pl.ds(HALO + 2, ne), :]
    p16 = sc_scr[pl.ds(m0 - 4, tt), :] + sc_scr[pl.ds(m0 + 4, tt), :]
    p2 = sa_scr[pl.ds(m0, tt), :]
    p4 = sb_scr[pl.ds(m0, tt), :]
    p8 = sc_scr[pl.ds(m0, tt), :]
    grp = lax.broadcasted_iota(jnp.int32, (tt, W_GROUP), 1) // (W_GROUP // len(POOL_WINDOWS))
    tpos = t0 + lax.broadcasted_iota(jnp.int32, (tt, W_GROUP), 0)
    sums = jnp.where(grp == 0, p2, jnp.where(grp == 1, p4, jnp.where(grp == 2, p8, p16)))
    half = jnp.where(grp == 0, 1, jnp.where(grp == 1, 2, jnp.where(grp == 2, 4, 8)))
    cnt = jnp.minimum(tpos + half, n_seq) - jnp.maximum(tpos - half, 0)
    bx = ext_scr[pl.ds(m0, tt), c0:c0 + W_GROUP]
    dpool = sums / cnt.astype(F32) - bx
    y_b = jnp.dot(dpool.astype(BF16), poolw_ref[...], preferred_element_type=F32) * pscale_ref[...]
    ybcd_ref[:, 0:W_GROUP] = y_b.astype(BF16)

    c_cb, c_cc, c_ch = 3 * W_GROUP, 4 * W_GROUP, 5 * W_GROUP
    sd_scr[HALO:HALO + ne, :] = (ext_scr[pl.ds(HALO, ne), c_cc:c_cc + W_GROUP]
                                 * ext_scr[pl.ds(HALO, ne), c_ch:c_ch + W_GROUP])
    cwc = cwc_ref[...]
    conv = jnp.zeros((tt, W_GROUP), F32)
    for k in range(3):
        conv = conv + cwc[k:k + 1, :] * sd_scr[pl.ds(m0 - 1 + k, tt), :]
    y_c = ext_scr[pl.ds(m0, tt), c_cb:c_cb + W_GROUP] * conv
    ybcd_ref[:, W_GROUP:2 * W_GROUP] = y_c.astype(BF16)

    c_du, c_dv = 6 * W_GROUP, 7 * W_GROUP
    u = _gelu(ext_scr[pl.ds(m0, tt), c_du:c_du + W_GROUP])
    gv = _gelu(ext_scr[pl.ds(m0, tt), c_dv:c_dv + W_GROUP])
    xc = gv - jnp.mean(gv, axis=-1, keepdims=True)
    v = xc * lax.rsqrt(jnp.mean(xc * xc, axis=-1, keepdims=True) + EPS) * sgug_ref[...]
    head = lax.broadcasted_iota(jnp.int32, (SGU_CHUNK, W_GROUP), 1) // (W_GROUP // SGU_HEADS)
    for c in range(tt // SGU_CHUNK):
        vc = v[c * SGU_CHUNK:(c + 1) * SGU_CHUNK, :]
        stack = jnp.concatenate([jnp.where(head == h, vc, 0.0) for h in range(SGU_HEADS)], axis=0)
        mixed = jnp.dot(sguw_ref[...], stack.astype(BF16), preferred_element_type=F32) + sgub_ref[...]
        y_d = u[c * SGU_CHUNK:(c + 1) * SGU_CHUNK, :] * mixed
        ybcd_ref[c * SGU_CHUNK:(c + 1) * SGU_CHUNK, 2 * W_GROUP:3 * W_GROUP] = y_d.astype(BF16)


def _mix_in_call(z, pos_tab, mods, lw, nct, tps, n_batch):
    n, d = z.shape
    tt = TOK_TILE
    nt = n // tt
    hb = tt // HALO
    n_pos = pos_tab.shape[0]

    def mrow(i):
        return jnp.where(i < nct, n_batch, (i - nct) // tps)

    def pblk(i):
        return jnp.where(i < nct, 0, 1 + (i - nct) % tps)

    full = lambda shape: pl.BlockSpec(shape, lambda i: (0,) * len(shape))
    in_specs = [
        pl.BlockSpec((HALO, d), lambda i: (jnp.maximum(i * hb - 1, 0), 0)),
        pl.BlockSpec((tt, d), lambda i: (i, 0)),
        pl.BlockSpec((HALO, d), lambda i: (jnp.minimum((i + 1) * hb, n // HALO - 1), 0)),
        pl.BlockSpec((HALO, d), lambda i: (jnp.maximum(pblk(i) * hb - 1, 0), 0)),
        pl.BlockSpec((tt, d), lambda i: (pblk(i), 0)),
        pl.BlockSpec((HALO, d), lambda i: (jnp.minimum((pblk(i) + 1) * hb, n_pos // HALO - 1), 0)),
        pl.BlockSpec((None, N_MOD, d), lambda i: (mrow(i), 0, 0)),
        full((1, d)),
        full((d, PROJ_DIM)),
        full((4, W_GROUP)), full((1, W_GROUP)), full((3, W_GROUP)),
        full((W_GROUP, W_GROUP)), full((1, W_GROUP)), full((1, W_GROUP)),
        full((SGU_CHUNK, SGU_HEADS * SGU_CHUNK)), full((SGU_CHUNK, W_GROUP)),
    ]
    out_specs = [pl.BlockSpec((tt, W_GROUP), lambda i: (i, 0)),
                 pl.BlockSpec((tt, W_GROUP), lambda i: (i, 0)),
                 pl.BlockSpec((tt, 3 * W_GROUP), lambda i: (i, 0))]
    out_shape = [jax.ShapeDtypeStruct((n, W_GROUP), F32),
                 jax.ShapeDtypeStruct((n, W_GROUP), F32),
                 jax.ShapeDtypeStruct((n, 3 * W_GROUP), BF16)]
    er = tt + 4 * HALO
    return pl.pallas_call(
        functools.partial(_mix_in_kernel, nct, tps),
        grid=(nt,),
        in_specs=in_specs, out_specs=out_specs, out_shape=out_shape,
        scratch_shapes=[pltpu.VMEM((er, PROJ_DIM), F32)] + [pltpu.VMEM((er, W_GROUP), F32)] * 4,
        compiler_params=pltpu.CompilerParams(vmem_limit_bytes=VMEM_LIMIT),
        name="mix_in",
    )(z, z, z, pos_tab, pos_tab, pos_tab, mods, lw["g_mix"], lw["w_in"], lw["conv_a_w"],
      lw["conv_a_b"], lw["conv_c_w"], lw["pool_w"], lw["pool_scale"], lw["sgu_g"],
      lw["sgu_w"], lw["sgu_b"])


def _chunk_scan(a, b, reverse):
    t = a.shape[0]
    row = lax.broadcasted_iota(jnp.int32, a.shape, 0)
    d = 1
    while d < t:
        shift = t - d if reverse else d
        ok = (row < t - d) if reverse else (row >= d)
        a_s = jnp.where(ok, pltpu.roll(a, shift, axis=0), 1.0)
        b_s = jnp.where(ok, pltpu.roll(b, shift, axis=0), 0.0)
        b = b + a * b_s
        a = a * a_s
        d *= 2
    return a, b


def _scan_kernel(xf_ref, xb_ref, wg_ref, bg_ref, lam_ref, hf_ref, hb_ref, carry_ref):
    tt = TOK_TILE
    s = pl.program_id(1)

    @pl.when(s == 0)
    def _():
        carry_ref[...] = jnp.zeros_like(carry_ref)

    for d, (x_ref, o_ref) in enumerate(((xf_ref, hf_ref), (xb_ref, hb_ref))):
        xa = x_ref[...]
        g = jnp.dot(xa.astype(BF16), wg_ref[d], preferred_element_type=F32) + bg_ref[d]
        r = jax.nn.sigmoid(g[:, 0:W_GROUP])
        gi = jax.nn.sigmoid(g[:, W_GROUP:2 * W_GROUP])
        neg_lam = -lam_ref[d:d + 1, :]
        softplus = jnp.maximum(neg_lam, 0.0) + jnp.log1p(jnp.exp(-jnp.abs(neg_lam)))
        log_a = (-LRU_C) * r * softplus
        a = jnp.exp(log_a)
        b = jnp.sqrt(1.0 - a * a) * (gi * xa)
        a_cum, h0 = _chunk_scan(a, b, reverse=(d == 1))
        h = h0 + a_cum * carry_ref[d:d + 1, :]
        o_ref[...] = h
        edge = 0 if d == 1 else tt - 1
        carry_ref[d:d + 1, :] = h[edge:edge + 1, :]


def _scan_call(xa, lw, nct, tps, n_batch):
    n, c = xa.shape
    tt = TOK_TILE
    fwd = lambda b, s: (jnp.where(s == 0, b, nct + b * tps + s - 1), 0)
    bwd = lambda b, s: (jnp.where(s == 0, b, nct + b * tps + tps - s), 0)
    full = lambda shape: pl.BlockSpec(shape, lambda b, s: (0,) * len(shape))
    return pl.pallas_call(
        _scan_kernel,
        grid=(n_batch, tps + 1),
        in_specs=[pl.BlockSpec((tt, c), fwd), pl.BlockSpec((tt, c), bwd),
                  full((2, c, 2 * c)), full((2, 1, 2 * c)), full((2, c))],
        out_specs=[pl.BlockSpec((tt, c), fwd), pl.BlockSpec((tt, c), bwd)],
        out_shape=[jax.ShapeDtypeStruct((n, c), F32)] * 2,
        scratch_shapes=[pltpu.VMEM((2, c), F32)],
        compiler_params=pltpu.CompilerParams(vmem_limit_bytes=VMEM_LIMIT),
        name="lru_scan",
    )(xa, xa, lw["lru_wg"], lw["lru_bg"], lw["lru_l"])


def _top16(cur, rank=None):
    rows = []
    for r in range(PEER_TOPK):
        mx = jnp.max(cur, axis=0, keepdims=True)
        rows.append(mx)
        hit = cur == mx
        cur = jnp.where(hit, NEG_BIG, cur)
        if rank is not None:
            rank = jnp.where(hit, float(r), rank)
    return rows, rank


def _mid_kernel(z_ref, hf_ref, hb_ref, gag_ref, ybcd_ref, mod_ref, wout_ref, gffn_ref, wqt_ref,
                k1_ref, k2_ref, zo_ref, ft_ref, c1_ref, e1_ref, r2_ref, e2_ref, qt_scr):
    ya = ((hf_ref[...] + hb_ref[...]) * gag_ref[...]).astype(BF16)
    y = jnp.concatenate([ya, ybcd_ref[...]], axis=1)
    o = jnp.dot(y, wout_ref[...], preferred_element_type=F32)
    z = z_ref[...] + mod_ref[2:3, :] * o
    zo_ref[...] = z
    f = _rms(z, gffn_ref[...]) * (1.0 + mod_ref[4:5, :]) + mod_ref[3:4, :]
    ft = f.T.astype(BF16)
    ft_ref[...] = ft
    qt_scr[...] = jnp.dot(wqt_ref[...], ft, preferred_element_type=F32)
    lanes = 128

    def head_body(h, carry):
        base = pl.multiple_of(h * (2 * PEER_HALF), 2 * PEER_HALF)
        q1 = qt_scr[pl.ds(base, PEER_HALF), :].astype(BF16)
        q2 = qt_scr[pl.ds(base + PEER_HALF, PEER_HALF), :].astype(BF16)
        s1_all = jnp.dot(k1_ref[...], q1, preferred_element_type=F32)
        s2_all = jnp.dot(k2_ref[...], q2, preferred_element_type=F32)
        for lt in range(TOK_TILE // lanes):
            sl = slice(lt * lanes, (lt + 1) * lanes)
            s1 = s1_all[:, sl]
            s2 = s2_all[:, sl]
            v1r, _ = _top16(s1)
            v2r, r2 = _top16(s2, jnp.full(s2.shape, float(PEER_TOPK), F32))
            v1 = jnp.concatenate(v1r, axis=0)
            v2 = jnp.concatenate(v2r, axis=0)
            blocks = [v2 + v1r[0]]
            for a in range(1, 8):
                blocks.append(v2[0:8, :] + v1r[a])
            blocks.append(v1[8:16, :] + v2r[0])
            tops, _ = _top16(jnp.concatenate(blocks, axis=0))
            m = tops[0]
            tau = tops[PEER_TOPK - 1]
            zsum = jnp.zeros_like(m)
            for row in tops:
                zsum = zsum + jnp.exp(row - m)
            theta = jnp.full(v2.shape, -NEG_BIG, F32)
            for a in range(PEER_TOPK):
                theta = jnp.minimum(theta, jnp.where(v1r[a] + v2 >= tau, v1r[a], -NEG_BIG))
            c1 = jnp.zeros(s1.shape, F32)
            for b in range(8):
                c1 = c1 + jnp.where(s1 >= theta[b:b + 1, :], 1.0, 0.0)
            n_hi = jnp.sum(jnp.where(theta[8:16, :] < 1.0e38, 1.0, 0.0), axis=0, keepdims=True)
            c1 = c1 + jnp.where(s1 >= v1r[0], n_hi, 0.0)
            c1_ref[h, :, sl] = c1
            e1_ref[h, :, sl] = jnp.exp(s1 - v1r[0]) * (1.0 / zsum)
            r2_ref[h, :, sl] = r2.astype(BF16)
            e2_ref[h, :, sl] = jnp.exp(s2 - v2r[0]).astype(BF16)
        return carry

    lax.fori_loop(0, PEER_HEADS, head_body, 0)


def _mid_call(z, hf, hb, gag, ybcd, mods, lw, nct, tps, n_batch):
    n, d = z.shape
    tt = TOK_TILE
    nt = n // tt
    nq = PEER_HEADS * 2 * PEER_HALF

    def mrow(i):
        return jnp.where(i < nct, n_batch, (i - nct) // tps)

    full = lambda shape: pl.BlockSpec(shape, lambda i: (0,) * len(shape))
    tok = lambda w: pl.BlockSpec((tt, w), lambda i: (i, 0))
    in_specs = [tok(d), tok(W_GROUP), tok(W_GROUP), tok(W_GROUP), tok(3 * W_GROUP),
                pl.BlockSpec((None, N_MOD, d), lambda i: (mrow(i), 0, 0)),
                full((d, d)), full((1, d)), full((nq, d)),
                full((N_KEYS, PEER_HALF)), full((N_KEYS, PEER_HALF))]
    hk = lambda: pl.BlockSpec((PEER_HEADS, N_KEYS, tt), lambda i: (0, 0, i))
    out_specs = [tok(d), pl.BlockSpec((d, tt), lambda i: (0, i)), hk(), hk(), hk(), hk()]
    hk_shape = lambda dt: jax.ShapeDtypeStruct((PEER_HEADS, N_KEYS, n), dt)
    out_shape = [jax.ShapeDtypeStruct((n, d), F32), jax.ShapeDtypeStruct((d, n), BF16),
                 hk_shape(F32), hk_shape(F32), hk_shape(BF16), hk_shape(BF16)]
    return pl.pallas_call(
        _mid_kernel,
        grid=(nt,),
        in_specs=in_specs, out_specs=out_specs, out_shape=out_shape,
        scratch_shapes=[pltpu.VMEM((nq, tt), F32)],
        compiler_params=pltpu.CompilerParams(vmem_limit_bytes=VMEM_LIMIT),
        name="mid",
    )(z, hf, hb, gag, ybcd, mods, lw["w_out"], lw["g_ffn"], lw["wq_t"], lw["k1"], lw["k2"])


def _peer_kernel(ft_ref, c1_ref, e1_ref, r2_ref, e2_ref, u_ref, vt_ref, z_ref, mod_ref,
                 zo_ref, acc_ref, a_scr, h_scr0, h_scr1, w_scr0, w_scr1):
    c = pl.program_id(1)
    tl = PEER_LANES
    sub = PEER_SUB
    pk = BF16_ROWS
    n_sub = PEER_ECHUNK // sub
    h_scr = (h_scr0, h_scr1)
    w_scr = (w_scr0, w_scr1)

    @pl.when(c == 0)
    def _():
        acc_ref[...] = jnp.zeros_like(acc_ref)

    def pre_act(k):
        h_scr[k % 2][...] = jnp.dot(u_ref[k * sub:(k + 1) * sub, :], ft_ref[...],
                                    preferred_element_type=F32)

    def gates(k):
        for il in range(sub // N_KEYS):
            i_loc = k * (sub // N_KEYS) + il
            for lt in range(PEER_TILE // tl):
                ls = slice(lt * tl, (lt + 1) * tl)
                w = jnp.zeros((N_KEYS // pk, pk, tl), BF16)
                for h in range(PEER_HEADS):
                    c1 = jnp.broadcast_to(c1_ref[h, i_loc:i_loc + 1, ls], (pk, tl)).astype(BF16)
                    e1 = jnp.broadcast_to(e1_ref[h, i_loc:i_loc + 1, ls], (pk, tl)).astype(BF16)
                    sel = jnp.where(r2_ref[h, :, :, ls] < c1[None], e2_ref[h, :, :, ls], 0.0)
                    w = w + sel * e1[None]
                w_scr[k % 2][il * N_KEYS:(il + 1) * N_KEYS, ls] = w.reshape(N_KEYS, tl)

    def activate(k):
        g = _gelu(h_scr[k % 2][...].astype(BF16))
        a_scr[k * sub:(k + 1) * sub, :] = g * w_scr[k % 2][...]

    def project(k):
        acc_ref[...] += jnp.dot(vt_ref[:, k * sub:(k + 1) * sub], a_scr[k * sub:(k + 1) * sub, :],
                                preferred_element_type=F32)

    pre_act(0)
    gates(0)
    for k in range(n_sub):
        if k + 1 < n_sub:
            pre_act(k + 1)
        if k >= 1:
            project(k - 1)
        activate(k)
        if k + 1 < n_sub:
            gates(k + 1)
    project(n_sub - 1)

    @pl.when(c == pl.num_programs(1) - 1)
    def _():
        zo_ref[...] = z_ref[...] + mod_ref[5:6, :] * acc_ref[...].T


def _peer_call(z, ft, c1, e1, r2, e2, mods, lw, n_ctx_tok, seq, n_batch):
    n, d = z.shape
    t = PEER_TILE
    ec = PEER_ECHUNK
    n_exp = lw["u"].shape[0]
    nct = n_ctx_tok // t
    tps = seq // t
    ic = ec // N_KEYS
    pk = BF16_ROWS
    r2 = r2.reshape(PEER_HEADS, N_KEYS // pk, pk, n)
    e2 = e2.reshape(PEER_HEADS, N_KEYS // pk, pk, n)

    def mrow(i):
        return jnp.where(i < nct, n_batch, (i - nct) // tps)

    in_specs = [
        pl.BlockSpec((d, t), lambda i, c: (0, i)),
        pl.BlockSpec((PEER_HEADS, ic, t), lambda i, c: (0, c, i)),
        pl.BlockSpec((PEER_HEADS, ic, t), lambda i, c: (0, c, i)),
        pl.BlockSpec((PEER_HEADS, N_KEYS // pk, pk, t), lambda i, c: (0, 0, 0, i)),
        pl.BlockSpec((PEER_HEADS, N_KEYS // pk, pk, t), lambda i, c: (0, 0, 0, i)),
        pl.BlockSpec((ec, d), lambda i, c: (c, 0)),
        pl.BlockSpec((d, ec), lambda i, c: (0, c)),
        pl.BlockSpec((t, d), lambda i, c: (i, 0)),
        pl.BlockSpec((None, N_MOD, d), lambda i, c: (mrow(i), 0, 0)),
    ]
    return pl.pallas_call(
        _peer_kernel,
        grid=(n // t, n_exp // ec),
        in_specs=in_specs,
        out_specs=pl.BlockSpec((t, d), lambda i, c: (i, 0)),
        out_shape=jax.ShapeDtypeStruct((n, d), F32),
        scratch_shapes=[pltpu.VMEM((d, t), F32), pltpu.VMEM((ec, t), BF16)]
        + [pltpu.VMEM((PEER_SUB, t), F32)] * 2 + [pltpu.VMEM((PEER_SUB, t), BF16)] * 2,
        compiler_params=pltpu.CompilerParams(vmem_limit_bytes=VMEM_LIMIT),
        name="peer",
    )(ft, c1, e1, r2, e2, lw["u"], lw["v_t"], z, mods)


def _final_kernel(z_ref, g_ref, o_ref):
    o_ref[...] = _rms(z_ref[...], g_ref[...])


def _final_call(z, g_final, n_ctx_tok):
    n, d = z.shape
    tt = TOK_TILE
    off = n_ctx_tok // tt
    return pl.pallas_call(
        _final_kernel,
        grid=((n - n_ctx_tok) // tt,),
        in_specs=[pl.BlockSpec((tt, d), lambda i: (i + off, 0)), pl.BlockSpec((1, d), lambda i: (0, 0))],
        out_specs=pl.BlockSpec((tt, d), lambda i: (i, 0)),
        out_shape=jax.ShapeDtypeStruct((n - n_ctx_tok, d), F32),
        name="final_norm",
    )(z, g_final.reshape(1, d))


def _block_diag(w):
    h, a, b = w.shape
    eye = jnp.eye(h, dtype=w.dtype)
    return (eye[:, None, :, None] * w[:, :, None, :]).reshape(h * a, h * b)


def _grid_sincos(n, dim):
    rows = n // GRID_W
    row = jnp.repeat(jnp.arange(rows, dtype=F32), GRID_W)
    col = jnp.tile(jnp.arange(GRID_W, dtype=F32), rows)
    quarter = dim // 4
    omega = POS_BASE ** (-jnp.arange(quarter, dtype=F32) / quarter)

    def axis_emb(p):
        ang = p[:, None] * omega[None, :]
        return jnp.concatenate([jnp.sin(ang), jnp.cos(ang)], axis=-1)

    return jnp.concatenate([axis_emb(row), axis_emb(col)], axis=-1)


def kernel(x, c, ctx, c_ctx, w_ada, b_ada, g_mix, g_ffn, w_in, w_out, conv_a_w, conv_a_b, lru_l, lru_wr, lru_br, lru_wi, lru_bi, pool_w, pool_scale, conv_c_w, sgu_g, sgu_w, sgu_b, peer_wq, peer_k1, peer_k2, peer_u, peer_v, g_final):
    bsz, seq, d = x.shape
    ctx_len = ctx.shape[1]
    depth = w_ada.shape[0]
    assert ctx_len == TOK_TILE and seq % TOK_TILE == 0 and seq % PEER_TILE == 0
    assert (bsz * ctx_len) % PEER_TILE == 0 and bsz + 1 <= 8
    n_ctx_tok = bsz * ctx_len
    nct = n_ctx_tok // TOK_TILE
    tps = seq // TOK_TILE

    cvec = jnp.concatenate([c, c_ctx[None, :], jnp.zeros((8 - bsz - 1, d), F32)], axis=0)
    mods_all = _ada_call(cvec, w_ada, b_ada).reshape(depth, 8, N_MOD, d)

    pos_tab = jnp.concatenate([jnp.zeros((TOK_TILE, d), F32), _grid_sincos(seq, d)], axis=0)
    z = jnp.concatenate([ctx.reshape(n_ctx_tok, d), x.reshape(bsz * seq, d)], axis=0)

    for l in range(depth):
        wg = jnp.stack([jnp.concatenate([_block_diag(lru_wr[l, dd]), _block_diag(lru_wi[l, dd])], axis=1)
                        for dd in range(2)]).astype(BF16)
        bg = jnp.stack([jnp.concatenate([lru_br[l, dd].reshape(1, -1), lru_bi[l, dd].reshape(1, -1)], axis=1)
                        for dd in range(2)])
        lw = {
            "g_mix": g_mix[l].reshape(1, d), "g_ffn": g_ffn[l].reshape(1, d),
            "w_in": w_in[l].astype(BF16), "w_out": w_out[l].astype(BF16),
            "conv_a_w": conv_a_w[l], "conv_a_b": conv_a_b[l].reshape(1, -1),
            "conv_c_w": conv_c_w[l],
            "pool_w": _block_diag(pool_w[l]).astype(BF16), "pool_scale": pool_scale[l].reshape(1, -1),
            "sgu_g": sgu_g[l].reshape(1, -1),
            "sgu_w": jnp.transpose(sgu_w[l], (1, 0, 2)).reshape(SGU_CHUNK, SGU_HEADS * SGU_CHUNK).astype(BF16),
            "sgu_b": jnp.repeat(sgu_b[l].T, W_GROUP // SGU_HEADS, axis=1),
            "lru_wg": wg, "lru_bg": bg, "lru_l": lru_l[l],
            "wq_t": peer_wq[l].T.astype(BF16), "k1": peer_k1[l].astype(BF16), "k2": peer_k2[l].astype(BF16),
            "u": peer_u[l].astype(BF16), "v_t": peer_v[l].T.astype(BF16),
        }
        mods = mods_all[l]
        xa, gag, ybcd = _mix_in_call(z, pos_tab, mods, lw, nct, tps, bsz)
        hf, hb = _scan_call(xa, lw, nct, tps, bsz)
        z, ft, c1, e1, r2, e2 = _mid_call(z, hf, hb, gag, ybcd, mods, lw, nct, tps, bsz)
        z = _peer_call(z, ft, c1, e1, r2, e2, mods, lw, n_ctx_tok, seq, bsz)

    out = _final_call(z, g_final, n_ctx_tok)
    return out.reshape(bsz, seq, d)
```

```python
import functools

import jax
import jax.numpy as jnp
from jax import lax
from jax.experimental import pallas as pl
from jax.experimental.pallas import tpu as pltpu

F32 = jnp.float32
BF16 = jnp.bfloat16

W_GROUP = 256
LRU_HEADS = 4
LRU_C = 8.0
POOL_WINDOWS = (2, 4, 8, 16)
SGU_CHUNK = 128
SGU_HEADS = 4
PROJ_DIM = 8 * W_GROUP
N_KEYS = 128
PEER_HEADS = 8
PEER_HALF = 128
PEER_TOPK = 16
N_MOD = 6
EPS = 1e-6
POS_BASE = 10000.0
GRID_W = 64

TOK_TILE = 256
HALO = 8
PEER_TILE = 512
PEER_ECHUNK = 2048
PEER_SUB = 512
PEER_LANES = 256
BF16_ROWS = 16
NEG_BIG = -3.0e38
VMEM_LIMIT = 56 * 1024 * 1024


def _gelu(x):
    return 0.5 * x * (1.0 + jnp.tanh(0.7978845608028654 * (x + 0.044715 * x * x * x)))


def _gelu_sigmoid(x):
    k1 = -2.0 * 0.7978845608028654 * 1.4426950408889634
    return x / (1.0 + jnp.exp2(x * (x * x * (k1 * 0.044715) + k1)))


def _rms(x, g):
    return x * lax.rsqrt(jnp.mean(x * x, axis=-1, keepdims=True) + EPS) * g


def _ada_kernel(c_ref, w_ref, b_ref, o_ref):
    cv = c_ref[...]
    s = cv * jax.nn.sigmoid(cv)
    o_ref[...] = jnp.dot(s.astype(BF16), w_ref[...].astype(BF16),
                         preferred_element_type=F32) + b_ref[...]


def _ada_call(cvec, w_ada, b_ada):
    depth, d, nm = w_ada.shape
    tn = 1536
    return pl.pallas_call(
        _ada_kernel,
        grid=(depth, nm // tn),
        in_specs=[pl.BlockSpec((8, d), lambda l, j: (0, 0)),
                  pl.BlockSpec((None, d, tn), lambda l, j: (l, 0, j)),
                  pl.BlockSpec((None, 1, tn), lambda l, j: (l, 0, j))],
        out_specs=pl.BlockSpec((None, 8, tn), lambda l, j: (l, 0, j)),
        out_shape=jax.ShapeDtypeStruct((depth, 8, nm), F32),
        compiler_params=pltpu.CompilerParams(vmem_limit_bytes=VMEM_LIMIT),
        name="ada",
    )(cvec, w_ada, b_ada.reshape(depth, 1, nm))


def _mix_in_kernel(nct, tps, zp_ref, z_ref, zn_ref, pp_ref, p_ref, pn_ref, mod_ref, gmix_ref,
                   win_ref, cwa_ref, cba_ref, cwc_ref, poolw_ref, pscale_ref, sgug_ref,
                   sguw_ref, sgub_ref, xa_ref, gag_ref, ybcd_ref,
                   ext_scr, sa_scr, sb_scr, sc_scr, sd_scr):
    tt = TOK_TILE
    i = pl.program_id(0)
    is_ctx = i < nct
    pos_tile = jnp.where(is_ctx, 0, (i - nct) % tps)
    seq_tiles = jnp.where(is_ctx, 1, tps)
    first = pos_tile == 0
    last = pos_tile == seq_tiles - 1
    t0 = pos_tile * tt
    n_seq = seq_tiles * tt

    z_ext = jnp.concatenate([zp_ref[...], z_ref[...], zn_ref[...]], axis=0)
    pos_ext = jnp.concatenate([pp_ref[...], p_ref[...], pn_ref[...]], axis=0)
    a = _rms(z_ext, gmix_ref[...])
    a = a * (1.0 + mod_ref[1:2, :]) + mod_ref[0:1, :] + pos_ext
    row = lax.broadcasted_iota(jnp.int32, (tt + 2 * HALO, 1), 0)
    keep_prev = jnp.where(first, 0.0, 1.0)
    keep_next = jnp.where(last, 0.0, 1.0)
    keep = jnp.where(row < HALO, keep_prev, jnp.where(row >= tt + HALO, keep_next, 1.0))
    a = a * keep
    p = jnp.dot(a.astype(BF16), win_ref[...], preferred_element_type=F32)

    zeros8 = jnp.zeros((HALO, PROJ_DIM), F32)
    ext_scr[0:HALO, :] = zeros8
    ext_scr[tt + 3 * HALO:tt + 4 * HALO, :] = zeros8
    ext_scr[HALO:tt + 3 * HALO, :] = p
    m0 = 2 * HALO
    ne = tt + 2 * HALO

    cwa = cwa_ref[...]
    xa = cba_ref[...]
    for k in range(4):
        xa = xa + cwa[k:k + 1, :] * ext_scr[pl.ds(m0 - 2 + k, tt), 0:W_GROUP]
    xa_ref[...] = xa
    gag_ref[...] = _gelu(ext_scr[pl.ds(m0, tt), W_GROUP:2 * W_GROUP])

    z8 = jnp.zeros((HALO, W_GROUP), F32)
    for scr in (sa_scr, sb_scr, sc_scr, sd_scr):
        scr[0:HALO, :] = z8
        scr[tt + 3 * HALO:tt + 4 * HALO, :] = z8
    c0 = 2 * W_GROUP
    sa_scr[HALO:HALO + ne, :] = (ext_scr[pl.ds(HALO - 1, ne), c0:c0 + W_GROUP]
                                 + ext_scr[pl.ds(HALO, ne), c0:c0 + W_GROUP])
    sb_scr[HALO:HALO + ne, :] = sa_scr[pl.ds(HALO - 1, ne), :] + sa_scr[pl.ds(HALO + 1, ne), :]
    sc_scr[HALO:HALO + ne, :] = sb_scr[pl.ds(HALO - 2, ne), :] + sb_scr[pl.ds(HALO + 2, ne), :]
    p16 = sc_scr[pl.ds(m0 - 4, tt), :] + sc_scr[pl.ds(m0 + 4, tt), :]
    p2 = sa_scr[pl.ds(m0, tt), :]
    p4 = sb_scr[pl.ds(m0, tt), :]
    p8 = sc_scr[pl.ds(m0, tt), :]
    grp = lax.broadcasted_iota(jnp.int32, (tt, W_GROUP), 1) // (W_GROUP // len(POOL_WINDOWS))
    tpos = t0 + lax.broadcasted_iota(jnp.int32, (tt, W_GROUP), 0)
    sums = jnp.where(grp == 0, p2, jnp.where(grp == 1, p4, jnp.where(grp == 2, p8, p16)))
    half = jnp.where(grp == 0, 1, jnp.where(grp == 1, 2, jnp.where(grp == 2, 4, 8)))
    cnt = jnp.minimum(tpos + half, n_seq) - jnp.maximum(tpos - half, 0)
    bx = ext_scr[pl.ds(m0, tt), c0:c0 + W_GROUP]
    dpool = sums / cnt.astype(F32) - bx
    y_b = jnp.dot(dpool.astype(BF16), poolw_ref[...], preferred_element_type=F32) * pscale_ref[...]
    ybcd_ref[:, 0:W_GROUP] = y_b.astype(BF16)

    c_cb, c_cc, c_ch = 3 * W_GROUP, 4 * W_GROUP, 5 * W_GROUP
    sd_scr[HALO:HALO + ne, :] = (ext_scr[pl.ds(HALO, ne), c_cc:c_cc + W_GROUP]
                                 * ext_scr[pl.ds(HALO, ne), c_ch:c_ch + W_GROUP])
    cwc = cwc_ref[...]
    conv = jnp.zeros((tt, W_GROUP), F32)
    for k in range(3):
        conv = conv + cwc[k:k + 1, :] * sd_scr[pl.ds(m0 - 1 + k, tt), :]
    y_c = ext_scr[pl.ds(m0, tt), c_cb:c_cb + W_GROUP] * conv
    ybcd_ref[:, W_GROUP:2 * W_GROUP] = y_c.astype(BF16)

    c_du, c_dv = 6 * W_GROUP, 7 * W_GROUP
    u = _gelu(ext_scr[pl.ds(m0, tt), c_du:c_du + W_GROUP])
    gv = _gelu(ext_scr[pl.ds(m0, tt), c_dv:c_dv + W_GROUP])
    xc = gv - jnp.mean(gv, axis=-1, keepdims=True)
    v = xc * lax.rsqrt(jnp.mean(xc * xc, axis=-1, keepdims=True) + EPS) * sgug_ref[...]
    head = lax.broadcasted_iota(jnp.int32, (SGU_CHUNK, W_GROUP), 1) // (W_GROUP // SGU_HEADS)
    for c in range(tt // SGU_CHUNK):
        vc = v[c * SGU_CHUNK:(c + 1) * SGU_CHUNK, :]
        stack = jnp.concatenate([jnp.where(head == h, vc, 0.0) for h in range(SGU_HEADS)], axis=0)
        mixed = jnp.dot(sguw_ref[...], stack.astype(BF16), preferred_element_type=F32) + sgub_ref[...]
        y_d = u[c * SGU_CHUNK:(c + 1) * SGU_CHUNK, :] * mixed
        ybcd_ref[c * SGU_CHUNK:(c + 1) * SGU_CHUNK, 2 * W_GROUP:3 * W_GROUP] = y_d.astype(BF16)


def _mix_in_call(z, pos_tab, mods, lw, nct, tps, n_batch):
    n, d = z.shape
    tt = TOK_TILE
    nt = n // tt
    hb = tt // HALO
    n_pos = pos_tab.shape[0]

    def mrow(i):
        return jnp.where(i < nct, n_batch, (i - nct) // tps)

    def pblk(i):
        return jnp.where(i < nct, 0, 1 + (i - nct) % tps)

    full = lambda shape: pl.BlockSpec(shape, lambda i: (0,) * len(shape))
    in_specs = [
        pl.BlockSpec((HALO, d), lambda i: (jnp.maximum(i * hb - 1, 0), 0)),
        pl.BlockSpec((tt, d), lambda i: (i, 0)),
        pl.BlockSpec((HALO, d), lambda i: (jnp.minimum((i + 1) * hb, n // HALO - 1), 0)),
        pl.BlockSpec((HALO, d), lambda i: (jnp.maximum(pblk(i) * hb - 1, 0), 0)),
        pl.BlockSpec((tt, d), lambda i: (pblk(i), 0)),
        pl.BlockSpec((HALO, d), lambda i: (jnp.minimum((pblk(i) + 1) * hb, n_pos // HALO - 1), 0)),
        pl.BlockSpec((None, N_MOD, d), lambda i: (mrow(i), 0, 0)),
        full((1, d)),
        full((d, PROJ_DIM)),
        full((4, W_GROUP)), full((1, W_GROUP)), full((3, W_GROUP)),
        full((W_GROUP, W_GROUP)), full((1, W_GROUP)), full((1, W_GROUP)),
        full((SGU_CHUNK, SGU_HEADS * SGU_CHUNK)), full((SGU_CHUNK, W_GROUP)),
    ]
    out_specs = [pl.BlockSpec((tt, W_GROUP), lambda i: (i, 0)),
                 pl.BlockSpec((tt, W_GROUP), lambda i: (i, 0)),
                 pl.BlockSpec((tt, 3 * W_GROUP), lambda i: (i, 0))]
    out_shape = [jax.ShapeDtypeStruct((n, W_GROUP), F32),
                 jax.ShapeDtypeStruct((n, W_GROUP), F32),
                 jax.ShapeDtypeStruct((n, 3 * W_GROUP), BF16)]
    er = tt + 4 * HALO
    return pl.pallas_call(
        functools.partial(_mix_in_kernel, nct, tps),
        grid=(nt,),
        in_specs=in_specs, out_specs=out_specs, out_shape=out_shape,
        scratch_shapes=[pltpu.VMEM((er, PROJ_DIM), F32)] + [pltpu.VMEM((er, W_GROUP), F32)] * 4,
        compiler_params=pltpu.CompilerParams(vmem_limit_bytes=VMEM_LIMIT),
        name="mix_in",
    )(z, z, z, pos_tab, pos_tab, pos_tab, mods, lw["g_mix"], lw["w_in"], lw["conv_a_w"],
      lw["conv_a_b"], lw["conv_c_w"], lw["pool_w"], lw["pool_scale"], lw["sgu_g"],
      lw["sgu_w"], lw["sgu_b"])


def _chunk_scan(a, b, reverse):
    t = a.shape[0]
    row = lax.broadcasted_iota(jnp.int32, a.shape, 0)
    d = 1
    while d < t:
        shift = t - d if reverse else d
        ok = (row < t - d) if reverse else (row >= d)
        a_s = jnp.where(ok, pltpu.roll(a, shift, axis=0), 1.0)
        b_s = jnp.where(ok, pltpu.roll(b, shift, axis=0), 0.0)
        b = b + a * b_s
        a = a * a_s
        d *= 2
    return a, b


def _scan_kernel(xf_ref, xb_ref, wg_ref, bg_ref, lam_ref, hf_ref, hb_ref, carry_ref):
    tt = TOK_TILE
    s = pl.program_id(1)

    @pl.when(s == 0)
    def _():
        carry_ref[...] = jnp.zeros_like(carry_ref)

    for d, (x_ref, o_ref) in enumerate(((xf_ref, hf_ref), (xb_ref, hb_ref))):
        xa = x_ref[...]
        g = jnp.dot(xa.astype(BF16), wg_ref[d], preferred_element_type=F32) + bg_ref[d]
        r = jax.nn.sigmoid(g[:, 0:W_GROUP])
        gi = jax.nn.sigmoid(g[:, W_GROUP:2 * W_GROUP])
        neg_lam = -lam_ref[d:d + 1, :]
        softplus = jnp.maximum(neg_lam, 0.0) + jnp.log1p(jnp.exp(-jnp.abs(neg_lam)))
        log_a = (-LRU_C) * r * softplus
        a = jnp.exp(log_a)
        b = jnp.sqrt(1.0 - a * a) * (gi * xa)
        a_cum, h0 = _chunk_scan(a, b, reverse=(d == 1))
        h = h0 + a_cum * carry_ref[d:d + 1, :]
        o_ref[...] = h
        edge = 0 if d == 1 else tt - 1
        carry_ref[d:d + 1, :] = h[edge:edge + 1, :]


def _scan_call(xa, lw, nct, tps, n_batch):
    n, c = xa.shape
    tt = TOK_TILE
    fwd = lambda b, s: (jnp.where(s == 0, b, nct + b * tps + s - 1), 0)
    bwd = lambda b, s: (jnp.where(s == 0, b, nct + b * tps + tps - s), 0)
    full = lambda shape: pl.BlockSpec(shape, lambda b, s: (0,) * len(shape))
    return pl.pallas_call(
        _scan_kernel,
        grid=(n_batch, tps + 1),
        in_specs=[pl.BlockSpec((tt, c), fwd), pl.BlockSpec((tt, c), bwd),
                  full((2, c, 2 * c)), full((2, 1, 2 * c)), full((2, c))],
        out_specs=[pl.BlockSpec((tt, c), fwd), pl.BlockSpec((tt, c), bwd)],
        out_shape=[jax.ShapeDtypeStruct((n, c), F32)] * 2,
        scratch_shapes=[pltpu.VMEM((2, c), F32)],
        compiler_params=pltpu.CompilerParams(vmem_limit_bytes=VMEM_LIMIT),
        name="lru_scan",
    )(xa, xa, lw["lru_wg"], lw["lru_bg"], lw["lru_l"])


def _sort16_network():
    def merge(lo, hi, r):
        step = r * 2
        if step < hi - lo:
            yield from merge(lo, hi, step)
            yield from merge(lo + r, hi, step)
            yield from [(i, i + r) for i in range(lo + r, hi - r, step)]
        else:
            yield (lo, lo + r)

    def sort(lo, hi):
        if hi - lo >= 1:
            mid = lo + (hi - lo) // 2
            yield from sort(lo, mid)
            yield from sort(mid + 1, hi)
            yield from merge(lo, hi, 1)

    return tuple(sort(0, PEER_TOPK - 1))


_SORT16 = _sort16_network()


def _top16(cur):
    rows = []
    for _ in range(PEER_TOPK):
        mx = jnp.max(cur, axis=0, keepdims=True)
        rows.append(mx)
        cur = jnp.where(cur == mx, NEG_BIG, cur)
    return rows


def _top16_of_keys(s):
    cols = [s[8 * g:8 * g + 8, :] for g in range(N_KEYS // 8)]
    for a, b in _SORT16:
        cols[a], cols[b] = jnp.maximum(cols[a], cols[b]), jnp.minimum(cols[a], cols[b])
    rows = []
    for r in range(PEER_TOPK):
        mx = jnp.max(cols[0], axis=0, keepdims=True)
        rows.append(mx)
        if r + 1 < PEER_TOPK:
            hit = cols[0] == mx
            depth = PEER_TOPK - r
            for d in range(depth - 1):
                cols[d] = jnp.where(hit, cols[d + 1], cols[d])
            cols[depth - 1] = jnp.where(hit, NEG_BIG, cols[depth - 1])
    return rows


def _rank16(s, v):
    m8 = s >= v[7]
    m4 = s >= jnp.where(m8, v[3], v[11])
    m2 = s >= jnp.where(m8, jnp.where(m4, v[1], v[5]), jnp.where(m4, v[9], v[13]))
    m1 = s >= jnp.where(m8, jnp.where(m4, jnp.where(m2, v[0], v[2]), jnp.where(m2, v[4], v[6])),
                        jnp.where(m4, jnp.where(m2, v[8], v[10]), jnp.where(m2, v[12], v[14])))
    rank = (jnp.where(m8, 0.0, 8.0) + jnp.where(m4, 0.0, 4.0)
            + jnp.where(m2, 0.0, 2.0) + jnp.where(m1, 0.0, 1.0))
    return jnp.where(s >= v[15], rank, float(PEER_TOPK))


def _mid_kernel(z_ref, hf_ref, hb_ref, gag_ref, ybcd_ref, mod_ref, wout_ref, gffn_ref, wqt_ref,
                k1_ref, k2_ref, zo_ref, ft_ref, c1_ref, e1_ref, r2_ref, e2_ref, qt_scr):
    ya = ((hf_ref[...] + hb_ref[...]) * gag_ref[...]).astype(BF16)
    y = jnp.concatenate([ya, ybcd_ref[...]], axis=1)
    o = jnp.dot(y, wout_ref[...], preferred_element_type=F32)
    z = z_ref[...] + mod_ref[2:3, :] * o
    zo_ref[...] = z
    f = _rms(z, gffn_ref[...]) * (1.0 + mod_ref[4:5, :]) + mod_ref[3:4, :]
    ft = f.T.astype(BF16)
    ft_ref[...] = ft
    qt_scr[...] = jnp.dot(wqt_ref[...], ft, preferred_element_type=F32)
    lanes = 128

    def head_body(h, carry):
        base = pl.multiple_of(h * (2 * PEER_HALF), 2 * PEER_HALF)
        q1 = qt_scr[pl.ds(base, PEER_HALF), :].astype(BF16)
        q2 = qt_scr[pl.ds(base + PEER_HALF, PEER_HALF), :].astype(BF16)
        s1_all = jnp.dot(k1_ref[...], q1, preferred_element_type=F32)
        s2_all = jnp.dot(k2_ref[...], q2, preferred_element_type=F32)
        for lt in range(TOK_TILE // lanes):
            sl = slice(lt * lanes, (lt + 1) * lanes)
            s1 = s1_all[:, sl]
            s2 = s2_all[:, sl]
            v1r = _top16_of_keys(s1)
            v2r = _top16_of_keys(s2)
            r2 = _rank16(s2, v2r)
            v1 = jnp.concatenate(v1r, axis=0)
            v2 = jnp.concatenate(v2r, axis=0)
            blocks = [v2 + v1r[0]]
            for a in range(1, 8):
                blocks.append(v2[0:8, :] + v1r[a])
            blocks.append(v1[8:16, :] + v2r[0])
            tops = _top16(jnp.concatenate(blocks, axis=0))
            m = tops[0]
            tau = tops[PEER_TOPK - 1]
            zsum = jnp.zeros_like(m)
            for row in tops:
                zsum = zsum + jnp.exp(row - m)
            theta = jnp.full(v2.shape, -NEG_BIG, F32)
            for a in range(PEER_TOPK):
                theta = jnp.minimum(theta, jnp.where(v1r[a] + v2 >= tau, v1r[a], -NEG_BIG))
            c1 = jnp.zeros(s1.shape, F32)
            for b in range(8):
                c1 = c1 + jnp.where(s1 >= theta[b:b + 1, :], 1.0, 0.0)
            n_hi = jnp.sum(jnp.where(theta[8:16, :] < 1.0e38, 1.0, 0.0), axis=0, keepdims=True)
            c1 = c1 + jnp.where(s1 >= v1r[0], n_hi, 0.0)
            c1_ref[h, :, sl] = c1
            e1_ref[h, :, sl] = jnp.exp(s1 - v1r[0]) * (1.0 / zsum)
            r2_ref[h, :, sl] = r2.astype(BF16)
            e2_ref[h, :, sl] = jnp.exp(s2 - v2r[0]).astype(BF16)
        return carry

    lax.fori_loop(0, PEER_HEADS, head_body, 0)


def _mid_call(z, hf, hb, gag, ybcd, mods, lw, nct, tps, n_batch, tile_off):
    n, d = z.shape
    tt = TOK_TILE
    nt = n // tt - tile_off
    nq = PEER_HEADS * 2 * PEER_HALF

    def mrow(i):
        j = i + tile_off
        return jnp.where(j < nct, n_batch, (j - nct) // tps)

    full = lambda shape: pl.BlockSpec(shape, lambda i: (0,) * len(shape))
    tok = lambda w: pl.BlockSpec((tt, w), lambda i: (i + tile_off, 0))
    in_specs = [tok(d), tok(W_GROUP), tok(W_GROUP), tok(W_GROUP), tok(3 * W_GROUP),
                pl.BlockSpec((None, N_MOD, d), lambda i: (mrow(i), 0, 0)),
                full((d, d)), full((1, d)), full((nq, d)),
                full((N_KEYS, PEER_HALF)), full((N_KEYS, PEER_HALF))]
    hk = lambda: pl.BlockSpec((PEER_HEADS, N_KEYS, tt), lambda i: (0, 0, i + tile_off))
    out_specs = [tok(d), pl.BlockSpec((d, tt), lambda i: (0, i + tile_off)), hk(), hk(), hk(), hk()]
    hk_shape = lambda dt: jax.ShapeDtypeStruct((PEER_HEADS, N_KEYS, n), dt)
    out_shape = [jax.ShapeDtypeStruct((n, d), F32), jax.ShapeDtypeStruct((d, n), BF16),
                 hk_shape(F32), hk_shape(F32), hk_shape(BF16), hk_shape(BF16)]
    return pl.pallas_call(
        _mid_kernel,
        grid=(nt,),
        in_specs=in_specs, out_specs=out_specs, out_shape=out_shape,
        scratch_shapes=[pltpu.VMEM((nq, tt), F32)],
        compiler_params=pltpu.CompilerParams(vmem_limit_bytes=VMEM_LIMIT),
        name="mid",
    )(z, hf, hb, gag, ybcd, mods, lw["w_out"], lw["g_ffn"], lw["wq_t"], lw["k1"], lw["k2"])


def _peer_kernel(final, ft_ref, c1_ref, e1_ref, r2_ref, e2_ref, u_ref, vt_ref, z_ref, mod_ref,
                 gfin_ref, zo_ref, acc_ref, a_scr, h_scr0, h_scr1, w_scr0, w_scr1):
    c = pl.program_id(1)
    tl = PEER_LANES
    sub = PEER_SUB
    pk = BF16_ROWS
    n_sub = PEER_ECHUNK // sub
    h_scr = (h_scr0, h_scr1)
    w_scr = (w_scr0, w_scr1)

    @pl.when(c == 0)
    def _():
        acc_ref[...] = jnp.zeros_like(acc_ref)

    def pre_act(k):
        h_scr[k % 2][...] = jnp.dot(u_ref[k * sub:(k + 1) * sub, :], ft_ref[...],
                                    preferred_element_type=F32)

    def gates(k):
        for il in range(sub // N_KEYS):
            i_loc = k * (sub // N_KEYS) + il
            for lt in range(PEER_TILE // tl):
                ls = slice(lt * tl, (lt + 1) * tl)
                w = jnp.zeros((N_KEYS // pk, pk, tl), BF16)
                for h in range(PEER_HEADS):
                    c1 = jnp.broadcast_to(c1_ref[h, i_loc:i_loc + 1, ls], (pk, tl)).astype(BF16)
                    e1 = jnp.broadcast_to(e1_ref[h, i_loc:i_loc + 1, ls], (pk, tl)).astype(BF16)
                    sel = jnp.where(r2_ref[h, :, :, ls] < c1[None], e2_ref[h, :, :, ls], 0.0)
                    w = w + sel * e1[None]
                w_scr[k % 2][il * N_KEYS:(il + 1) * N_KEYS, ls] = w.reshape(N_KEYS, tl)

    def activate(k):
        g = _gelu_sigmoid(h_scr[k % 2][...].astype(BF16))
        a_scr[k * sub:(k + 1) * sub, :] = g * w_scr[k % 2][...]

    def project(k):
        acc_ref[...] += jnp.dot(vt_ref[:, k * sub:(k + 1) * sub], a_scr[k * sub:(k + 1) * sub, :],
                                preferred_element_type=F32)

    pre_act(0)
    gates(0)
    for k in range(n_sub):
        if k + 1 < n_sub:
            pre_act(k + 1)
        if k >= 1:
            project(k - 1)
        activate(k)
        if k + 1 < n_sub:
            gates(k + 1)
    project(n_sub - 1)

    @pl.when(c == pl.num_programs(1) - 1)
    def _():
        z = z_ref[...] + mod_ref[5:6, :] * acc_ref[...].T
        zo_ref[...] = _rms(z, gfin_ref[...]) if final else z


def _peer_call(z, ft, c1, e1, r2, e2, mods, lw, g_final, n_ctx_tok, seq, n_batch, final):
    n, d = z.shape
    t = PEER_TILE
    ec = PEER_ECHUNK
    n_exp = lw["u"].shape[0]
    nct = n_ctx_tok // t
    tps = seq // t
    ic = ec // N_KEYS
    pk = BF16_ROWS
    off = nct if final else 0
    r2 = r2.reshape(PEER_HEADS, N_KEYS // pk, pk, n)
    e2 = e2.reshape(PEER_HEADS, N_KEYS // pk, pk, n)

    def mrow(i):
        j = i + off
        return jnp.where(j < nct, n_batch, (j - nct) // tps)

    in_specs = [
        pl.BlockSpec((d, t), lambda i, c: (0, i + off)),
        pl.BlockSpec((PEER_HEADS, ic, t), lambda i, c: (0, c, i + off)),
        pl.BlockSpec((PEER_HEADS, ic, t), lambda i, c: (0, c, i + off)),
        pl.BlockSpec((PEER_HEADS, N_KEYS // pk, pk, t), lambda i, c: (0, 0, 0, i + off)),
        pl.BlockSpec((PEER_HEADS, N_KEYS // pk, pk, t), lambda i, c: (0, 0, 0, i + off)),
        pl.BlockSpec((ec, d), lambda i, c: (c, 0)),
        pl.BlockSpec((d, ec), lambda i, c: (0, c)),
        pl.BlockSpec((t, d), lambda i, c: (i + off, 0)),
        pl.BlockSpec((None, N_MOD, d), lambda i, c: (mrow(i), 0, 0)),
        pl.BlockSpec((1, d), lambda i, c: (0, 0)),
    ]
    return pl.pallas_call(
        functools.partial(_peer_kernel, final),
        grid=(n // t - off, n_exp // ec),
        in_specs=in_specs,
        out_specs=pl.BlockSpec((t, d), lambda i, c: (i, 0)),
        out_shape=jax.ShapeDtypeStruct((n - off * t, d), F32),
        scratch_shapes=[pltpu.VMEM((d, t), F32), pltpu.VMEM((ec, t), BF16)]
        + [pltpu.VMEM((PEER_SUB, t), F32)] * 2 + [pltpu.VMEM((PEER_SUB, t), BF16)] * 2,
        compiler_params=pltpu.CompilerParams(vmem_limit_bytes=VMEM_LIMIT),
        name="peer",
    )(ft, c1, e1, r2, e2, lw["u"], lw["v_t"], z, mods, g_final)


def _block_diag(w):
    h, a, b = w.shape
    eye = jnp.eye(h, dtype=w.dtype)
    return (eye[:, None, :, None] * w[:, :, None, :]).reshape(h * a, h * b)


def _grid_sincos(n, dim):
    rows = n // GRID_W
    row = jnp.repeat(jnp.arange(rows, dtype=F32), GRID_W)
    col = jnp.tile(jnp.arange(GRID_W, dtype=F32), rows)
    quarter = dim // 4
    omega = POS_BASE ** (-jnp.arange(quarter, dtype=F32) / quarter)

    def axis_emb(p):
        ang = p[:, None] * omega[None, :]
        return jnp.concatenate([jnp.sin(ang), jnp.cos(ang)], axis=-1)

    return jnp.concatenate([axis_emb(row), axis_emb(col)], axis=-1)


def kernel(x, c, ctx, c_ctx, w_ada, b_ada, g_mix, g_ffn, w_in, w_out, conv_a_w, conv_a_b, lru_l, lru_wr, lru_br, lru_wi, lru_bi, pool_w, pool_scale, conv_c_w, sgu_g, sgu_w, sgu_b, peer_wq, peer_k1, peer_k2, peer_u, peer_v, g_final):
    bsz, seq, d = x.shape
    ctx_len = ctx.shape[1]
    depth = w_ada.shape[0]
    assert ctx_len == TOK_TILE and seq % TOK_TILE == 0 and seq % PEER_TILE == 0
    assert (bsz * ctx_len) % PEER_TILE == 0 and bsz + 1 <= 8
    n_ctx_tok = bsz * ctx_len
    nct = n_ctx_tok // TOK_TILE
    tps = seq // TOK_TILE

    cvec = jnp.concatenate([c, c_ctx[None, :], jnp.zeros((8 - bsz - 1, d), F32)], axis=0)
    mods_all = _ada_call(cvec, w_ada, b_ada).reshape(depth, 8, N_MOD, d)

    pos_tab = jnp.concatenate([jnp.zeros((TOK_TILE, d), F32), _grid_sincos(seq, d)], axis=0)
    z = jnp.concatenate([ctx.reshape(n_ctx_tok, d), x.reshape(bsz * seq, d)], axis=0)

    for l in range(depth):
        wg = jnp.stack([jnp.concatenate([_block_diag(lru_wr[l, dd]), _block_diag(lru_wi[l, dd])], axis=1)
                        for dd in range(2)]).astype(BF16)
        bg = jnp.stack([jnp.concatenate([lru_br[l, dd].reshape(1, -1), lru_bi[l, dd].reshape(1, -1)], axis=1)
                        for dd in range(2)])
        lw = {
            "g_mix": g_mix[l].reshape(1, d), "g_ffn": g_ffn[l].reshape(1, d),
            "w_in": w_in[l].astype(BF16), "w_out": w_out[l].astype(BF16),
            "conv_a_w": conv_a_w[l], "conv_a_b": conv_a_b[l].reshape(1, -1),
            "conv_c_w": conv_c_w[l],
            "pool_w": _block_diag(pool_w[l]).astype(BF16), "pool_scale": pool_scale[l].reshape(1, -1),
            "sgu_g": sgu_g[l].reshape(1, -1),
            "sgu_w": jnp.transpose(sgu_w[l], (1, 0, 2)).reshape(SGU_CHUNK, SGU_HEADS * SGU_CHUNK).astype(BF16),
            "sgu_b": jnp.repeat(sgu_b[l].T, W_GROUP // SGU_HEADS, axis=1),
            "lru_wg": wg, "lru_bg": bg, "lru_l": lru_l[l],
            "wq_t": peer_wq[l].T.astype(BF16), "k1": peer_k1[l].astype(BF16), "k2": peer_k2[l].astype(BF16),
            "u": peer_u[l].astype(BF16), "v_t": peer_v[l].T.astype(BF16),
        }
        mods = mods_all[l]
        xa, gag, ybcd = _mix_in_call(z, pos_tab, mods, lw, nct, tps, bsz)
        hf, hb = _scan_call(xa, lw, nct, tps, bsz)
        last = l == depth - 1
        z, ft, c1, e1, r2, e2 = _mid_call(z, hf, hb, gag, ybcd, mods, lw, nct, tps, bsz,
                                          nct if last else 0)
        z = _peer_call(z, ft, c1, e1, r2, e2, mods, lw, g_final.reshape(1, d), n_ctx_tok, seq, bsz,
                       last)

    return z.reshape(bsz, seq, d)
```

```python
import functools

import jax
import jax.numpy as jnp
from jax import lax
from jax.experimental import pallas as pl
from jax.experimental.pallas import tpu as pltpu

F32 = jnp.float32
BF16 = jnp.bfloat16

W_GROUP = 256
LRU_HEADS = 4
LRU_C = 8.0
POOL_WINDOWS = (2, 4, 8, 16)
SGU_CHUNK = 128
SGU_HEADS = 4
PROJ_DIM = 8 * W_GROUP
N_KEYS = 128
PEER_HEADS = 8
PEER_HALF = 128
PEER_TOPK = 16
N_MOD = 6
EPS = 1e-6
POS_BASE = 10000.0
GRID_W = 64

TOK_TILE = 256
MID_TILE = 512
HALO = 8
PEER_TILE = 512
PEER_ECHUNK = 2048
PEER_SUB = 512
PEER_LANES = 256
BF16_ROWS = 16
NEG_BIG = -3.0e38
VMEM_LIMIT = 56 * 1024 * 1024


def _gelu(x):
    return 0.5 * x * (1.0 + jnp.tanh(0.7978845608028654 * (x + 0.044715 * x * x * x)))


def _gelu_sigmoid(x):
    k1 = -2.0 * 0.7978845608028654 * 1.4426950408889634
    return x / (1.0 + jnp.exp2(x * (x * x * (k1 * 0.044715) + k1)))


def _rms(x, g):
    return x * lax.rsqrt(jnp.mean(x * x, axis=-1, keepdims=True) + EPS) * g


def _ada_kernel(c_ref, w_ref, b_ref, o_ref):
    cv = c_ref[...]
    s = cv * jax.nn.sigmoid(cv)
    o_ref[...] = jnp.dot(s.astype(BF16), w_ref[...].astype(BF16),
                         preferred_element_type=F32) + b_ref[...]


def _ada_call(cvec, w_ada, b_ada):
    depth, d, nm = w_ada.shape
    tn = 1536
    return pl.pallas_call(
        _ada_kernel,
        grid=(depth, nm // tn),
        in_specs=[pl.BlockSpec((8, d), lambda l, j: (0, 0)),
                  pl.BlockSpec((None, d, tn), lambda l, j: (l, 0, j)),
                  pl.BlockSpec((None, 1, tn), lambda l, j: (l, 0, j))],
        out_specs=pl.BlockSpec((None, 8, tn), lambda l, j: (l, 0, j)),
        out_shape=jax.ShapeDtypeStruct((depth, 8, nm), F32),
        compiler_params=pltpu.CompilerParams(vmem_limit_bytes=VMEM_LIMIT),
        name="ada",
    )(cvec, w_ada, b_ada.reshape(depth, 1, nm))


def _mix_in_kernel(nct, tps, zp_ref, z_ref, zn_ref, pp_ref, p_ref, pn_ref, mod_ref, gmix_ref,
                   win_ref, cwa_ref, cba_ref, cwc_ref, poolw_ref, pscale_ref, sgug_ref,
                   sguw_ref, sgub_ref, xa_ref, gag_ref, ybcd_ref,
                   ext_scr, sa_scr, sb_scr, sc_scr, sd_scr):
    tt = TOK_TILE
    i = pl.program_id(0)
    is_ctx = i < nct
    pos_tile = jnp.where(is_ctx, 0, (i - nct) % tps)
    seq_tiles = jnp.where(is_ctx, 1, tps)
    first = pos_tile == 0
    last = pos_tile == seq_tiles - 1
    t0 = pos_tile * tt
    n_seq = seq_tiles * tt

    z_ext = jnp.concatenate([zp_ref[...], z_ref[...], zn_ref[...]], axis=0)
    pos_ext = jnp.concatenate([pp_ref[...], p_ref[...], pn_ref[...]], axis=0)
    a = _rms(z_ext, gmix_ref[...])
    a = a * (1.0 + mod_ref[1:2, :]) + mod_ref[0:1, :] + pos_ext
    row = lax.broadcasted_iota(jnp.int32, (tt + 2 * HALO, 1), 0)
    keep_prev = jnp.where(first, 0.0, 1.0)
    keep_next = jnp.where(last, 0.0, 1.0)
    keep = jnp.where(row < HALO, keep_prev, jnp.where(row >= tt + HALO, keep_next, 1.0))
    a = a * keep
    p = jnp.dot(a.astype(BF16), win_ref[...], preferred_element_type=F32)

    zeros8 = jnp.zeros((HALO, PROJ_DIM), F32)
    ext_scr[0:HALO, :] = zeros8
    ext_scr[tt + 3 * HALO:tt + 4 * HALO, :] = zeros8
    ext_scr[HALO:tt + 3 * HALO, :] = p
    m0 = 2 * HALO
    ne = tt + 2 * HALO

    cwa = cwa_ref[...]
    xa = cba_ref[...]
    for k in range(4):
        xa = xa + cwa[k:k + 1, :] * ext_scr[pl.ds(m0 - 2 + k, tt), 0:W_GROUP]
    xa_ref[...] = xa
    gag_ref[...] = _gelu(ext_scr[pl.ds(m0, tt), W_GROUP:2 * W_GROUP])

    z8 = jnp.zeros((HALO, W_GROUP), F32)
    for scr in (sa_scr, sb_scr, sc_scr, sd_scr):
        scr[0:HALO, :] = z8
        scr[tt + 3 * HALO:tt + 4 * HALO, :] = z8
    c0 = 2 * W_GROUP
    sa_scr[HALO:HALO + ne, :] = (ext_scr[pl.ds(HALO - 1, ne), c0:c0 + W_GROUP]
                                 + ext_scr[pl.ds(HALO, ne), c0:c0 + W_GROUP])
    sb_scr[HALO:HALO + ne, :] = sa_scr[pl.ds(HALO - 1, ne), :] + sa_scr[pl.ds(HALO + 1, ne), :]
    sc_scr[HALO:HALO + ne, :] = sb_scr[pl.ds(HALO - 2, ne), :] + sb_scr[pl.ds(HALO + 2, ne), :]
    p16 = sc_scr[pl.ds(m0 - 4, tt), :] + sc_scr[pl.ds(m0 + 4, tt), :]
    p2 = sa_scr[pl.ds(m0, tt), :]
    p4 = sb_scr[pl.ds(m0, tt), :]
    p8 = sc_scr[pl.ds(m0, tt), :]
    grp = lax.broadcasted_iota(jnp.int32, (tt, W_GROUP), 1) // (W_GROUP // len(POOL_WINDOWS))
    tpos = t0 + lax.broadcasted_iota(jnp.int32, (tt, W_GROUP), 0)
    sums = jnp.where(grp == 0, p2, jnp.where(grp == 1, p4, jnp.where(grp == 2, p8, p16)))
    half = jnp.where(grp == 0, 1, jnp.where(grp == 1, 2, jnp.where(grp == 2, 4, 8)))
    cnt = jnp.minimum(tpos + half, n_seq) - jnp.maximum(tpos - half, 0)
    bx = ext_scr[pl.ds(m0, tt), c0:c0 + W_GROUP]
    dpool = sums / cnt.astype(F32) - bx
    y_b = jnp.dot(dpool.astype(BF16), poolw_ref[...], preferred_element_type=F32) * pscale_ref[...]
    ybcd_ref[:, 0:W_GROUP] = y_b.astype(BF16)

    c_cb, c_cc, c_ch = 3 * W_GROUP, 4 * W_GROUP, 5 * W_GROUP
    sd_scr[HALO:HALO + ne, :] = (ext_scr[pl.ds(HALO, ne), c_cc:c_cc + W_GROUP]
                                 * ext_scr[pl.ds(HALO, ne), c_ch:c_ch + W_GROUP])
    cwc = cwc_ref[...]
    conv = jnp.zeros((tt, W_GROUP), F32)
    for k in range(3):
        conv = conv + cwc[k:k + 1, :] * sd_scr[pl.ds(m0 - 1 + k, tt), :]
    y_c = ext_scr[pl.ds(m0, tt), c_cb:c_cb + W_GROUP] * conv
    ybcd_ref[:, W_GROUP:2 * W_GROUP] = y_c.astype(BF16)

    c_du, c_dv = 6 * W_GROUP, 7 * W_GROUP
    u = _gelu(ext_scr[pl.ds(m0, tt), c_du:c_du + W_GROUP])
    gv = _gelu(ext_scr[pl.ds(m0, tt), c_dv:c_dv + W_GROUP])
    xc = gv - jnp.mean(gv, axis=-1, keepdims=True)
    v = xc * lax.rsqrt(jnp.mean(xc * xc, axis=-1, keepdims=True) + EPS) * sgug_ref[...]
    head = lax.broadcasted_iota(jnp.int32, (SGU_CHUNK, W_GROUP), 1) // (W_GROUP // SGU_HEADS)
    for c in range(tt // SGU_CHUNK):
        vc = v[c * SGU_CHUNK:(c + 1) * SGU_CHUNK, :]
        stack = jnp.concatenate([jnp.where(head == h, vc, 0.0) for h in range(SGU_HEADS)], axis=0)
        mixed = jnp.dot(sguw_ref[...], stack.astype(BF16), preferred_element_type=F32) + sgub_ref[...]
        y_d = u[c * SGU_CHUNK:(c + 1) * SGU_CHUNK, :] * mixed
        ybcd_ref[c * SGU_CHUNK:(c + 1) * SGU_CHUNK, 2 * W_GROUP:3 * W_GROUP] = y_d.astype(BF16)


def _mix_in_call(z, pos_tab, mods, lw, layer, nct, tps, n_batch):
    n, d = z.shape
    tt = TOK_TILE
    nt = n // tt
    hb = tt // HALO
    n_pos = pos_tab.shape[0]

    def mrow(i):
        return jnp.where(i < nct, n_batch, (i - nct) // tps)

    def pblk(i):
        return jnp.where(i < nct, 0, 1 + (i - nct) % tps)

    full = lambda shape: pl.BlockSpec(shape, lambda i: (0,) * len(shape))
    in_specs = [
        pl.BlockSpec((HALO, d), lambda i: (jnp.maximum(i * hb - 1, 0), 0)),
        pl.BlockSpec((tt, d), lambda i: (i, 0)),
        pl.BlockSpec((HALO, d), lambda i: (jnp.minimum((i + 1) * hb, n // HALO - 1), 0)),
        pl.BlockSpec((HALO, d), lambda i: (jnp.maximum(pblk(i) * hb - 1, 0), 0)),
        pl.BlockSpec((tt, d), lambda i: (pblk(i), 0)),
        pl.BlockSpec((HALO, d), lambda i: (jnp.minimum((pblk(i) + 1) * hb, n_pos // HALO - 1), 0)),
        pl.BlockSpec((None, N_MOD, d), lambda i: (mrow(i), 0, 0)),
        full((1, d)),
        pl.BlockSpec((None, d, PROJ_DIM), lambda i: (layer, 0, 0)),
        full((4, W_GROUP)), full((1, W_GROUP)), full((3, W_GROUP)),
        full((W_GROUP, W_GROUP)), full((1, W_GROUP)), full((1, W_GROUP)),
        full((SGU_CHUNK, SGU_HEADS * SGU_CHUNK)), full((SGU_CHUNK, W_GROUP)),
    ]
    out_specs = [pl.BlockSpec((tt, W_GROUP), lambda i: (i, 0)),
                 pl.BlockSpec((tt, W_GROUP), lambda i: (i, 0)),
                 pl.BlockSpec((tt, 3 * W_GROUP), lambda i: (i, 0))]
    out_shape = [jax.ShapeDtypeStruct((n, W_GROUP), F32),
                 jax.ShapeDtypeStruct((n, W_GROUP), F32),
                 jax.ShapeDtypeStruct((n, 3 * W_GROUP), BF16)]
    er = tt + 4 * HALO
    return pl.pallas_call(
        functools.partial(_mix_in_kernel, nct, tps),
        grid=(nt,),
        in_specs=in_specs, out_specs=out_specs, out_shape=out_shape,
        scratch_shapes=[pltpu.VMEM((er, PROJ_DIM), F32)] + [pltpu.VMEM((er, W_GROUP), F32)] * 4,
        compiler_params=pltpu.CompilerParams(vmem_limit_bytes=VMEM_LIMIT),
        name="mix_in",
    )(z, z, z, pos_tab, pos_tab, pos_tab, mods, lw["g_mix"], lw["w_in"], lw["conv_a_w"],
      lw["conv_a_b"], lw["conv_c_w"], lw["pool_w"], lw["pool_scale"], lw["sgu_g"],
      lw["sgu_w"], lw["sgu_b"])


def _chunk_scan(a, b, reverse):
    t = a.shape[0]
    row = lax.broadcasted_iota(jnp.int32, a.shape, 0)
    d = 1
    while d < t:
        shift = t - d if reverse else d
        ok = (row < t - d) if reverse else (row >= d)
        a_s = jnp.where(ok, pltpu.roll(a, shift, axis=0), 1.0)
        b_s = jnp.where(ok, pltpu.roll(b, shift, axis=0), 0.0)
        b = b + a * b_s
        a = a * a_s
        d *= 2
    return a, b


def _scan_kernel(xf_ref, xb_ref, wg_ref, bg_ref, lam_ref, hf_ref, hb_ref, carry_ref):
    tt = TOK_TILE
    s = pl.program_id(1)

    @pl.when(s == 0)
    def _():
        carry_ref[...] = jnp.zeros_like(carry_ref)

    for d, (x_ref, o_ref) in enumerate(((xf_ref, hf_ref), (xb_ref, hb_ref))):
        xa = x_ref[...]
        g = jnp.dot(xa.astype(BF16), wg_ref[d], preferred_element_type=F32) + bg_ref[d]
        r = jax.nn.sigmoid(g[:, 0:W_GROUP])
        gi = jax.nn.sigmoid(g[:, W_GROUP:2 * W_GROUP])
        neg_lam = -lam_ref[d:d + 1, :]
        softplus = jnp.maximum(neg_lam, 0.0) + jnp.log1p(jnp.exp(-jnp.abs(neg_lam)))
        log_a = (-LRU_C) * r * softplus
        a = jnp.exp(log_a)
        b = jnp.sqrt(1.0 - a * a) * (gi * xa)
        a_cum, h0 = _chunk_scan(a, b, reverse=(d == 1))
        h = h0 + a_cum * carry_ref[d:d + 1, :]
        o_ref[...] = h
        edge = 0 if d == 1 else tt - 1
        carry_ref[d:d + 1, :] = h[edge:edge + 1, :]


def _scan_call(xa, lw, nct, tps, n_batch):
    n, c = xa.shape
    tt = TOK_TILE
    fwd = lambda b, s: (jnp.where(s == 0, b, nct + b * tps + s - 1), 0)
    bwd = lambda b, s: (jnp.where(s == 0, b, nct + b * tps + tps - s), 0)
    full = lambda shape: pl.BlockSpec(shape, lambda b, s: (0,) * len(shape))
    return pl.pallas_call(
        _scan_kernel,
        grid=(n_batch, tps + 1),
        in_specs=[pl.BlockSpec((tt, c), fwd), pl.BlockSpec((tt, c), bwd),
                  full((2, c, 2 * c)), full((2, 1, 2 * c)), full((2, c))],
        out_specs=[pl.BlockSpec((tt, c), fwd), pl.BlockSpec((tt, c), bwd)],
        out_shape=[jax.ShapeDtypeStruct((n, c), F32)] * 2,
        scratch_shapes=[pltpu.VMEM((2, c), F32)],
        compiler_params=pltpu.CompilerParams(vmem_limit_bytes=VMEM_LIMIT),
        name="lru_scan",
    )(xa, xa, lw["lru_wg"], lw["lru_bg"], lw["lru_l"])


def _sort16_network():
    def merge(lo, hi, r):
        step = r * 2
        if step < hi - lo:
            yield from merge(lo, hi, step)
            yield from merge(lo + r, hi, step)
            yield from [(i, i + r) for i in range(lo + r, hi - r, step)]
        else:
            yield (lo, lo + r)

    def sort(lo, hi):
        if hi - lo >= 1:
            mid = lo + (hi - lo) // 2
            yield from sort(lo, mid)
            yield from sort(mid + 1, hi)
            yield from merge(lo, hi, 1)

    return tuple(sort(0, PEER_TOPK - 1))


_SORT16 = _sort16_network()


def _top16(cur):
    rows = []
    for _ in range(PEER_TOPK):
        mx = jnp.max(cur, axis=0, keepdims=True)
        rows.append(mx)
        cur = jnp.where(cur == mx, NEG_BIG, cur)
    return rows


def _top16_of_keys(s):
    cols = [s[8 * g:8 * g + 8, :] for g in range(N_KEYS // 8)]
    for a, b in _SORT16:
        cols[a], cols[b] = jnp.maximum(cols[a], cols[b]), jnp.minimum(cols[a], cols[b])
    rows = []
    for r in range(PEER_TOPK):
        mx = jnp.max(cols[0], axis=0, keepdims=True)
        rows.append(mx)
        if r + 1 < PEER_TOPK:
            hit = cols[0] == mx
            depth = PEER_TOPK - r
            for d in range(depth - 1):
                cols[d] = jnp.where(hit, cols[d + 1], cols[d])
            cols[depth - 1] = jnp.where(hit, NEG_BIG, cols[depth - 1])
    return rows


def _rank16(s, v):
    m8 = s >= v[7]
    m4 = s >= jnp.where(m8, v[3], v[11])
    m2 = s >= jnp.where(m8, jnp.where(m4, v[1], v[5]), jnp.where(m4, v[9], v[13]))
    m1 = s >= jnp.where(m8, jnp.where(m4, jnp.where(m2, v[0], v[2]), jnp.where(m2, v[4], v[6])),
                        jnp.where(m4, jnp.where(m2, v[8], v[10]), jnp.where(m2, v[12], v[14])))
    rank = (jnp.where(m8, 0.0, 8.0) + jnp.where(m4, 0.0, 4.0)
            + jnp.where(m2, 0.0, 2.0) + jnp.where(m1, 0.0, 1.0))
    return jnp.where(s >= v[15], rank, float(PEER_TOPK))


def _mid_kernel(z_ref, hf_ref, hb_ref, gag_ref, ybcd_ref, mod_ref, wout_ref, gffn_ref, wqt_ref,
                k1_ref, k2_ref, zo_ref, ft_ref, c1_ref, e1_ref, r2_ref, e2_ref, qt_scr):
    ya = ((hf_ref[...] + hb_ref[...]) * gag_ref[...]).astype(BF16)
    y = jnp.concatenate([ya, ybcd_ref[...]], axis=1)
    o = jnp.dot(y, wout_ref[...], preferred_element_type=F32)
    z = z_ref[...] + mod_ref[2:3, :] * o
    zo_ref[...] = z
    f = _rms(z, gffn_ref[...]) * (1.0 + mod_ref[4:5, :]) + mod_ref[3:4, :]
    ft = f.T.astype(BF16)
    ft_ref[...] = ft
    qt_scr[...] = jnp.dot(wqt_ref[...], ft, preferred_element_type=F32)
    lanes = 128

    def head_body(h, carry):
        base = pl.multiple_of(h * (2 * PEER_HALF), 2 * PEER_HALF)
        q1 = qt_scr[pl.ds(base, PEER_HALF), :].astype(BF16)
        q2 = qt_scr[pl.ds(base + PEER_HALF, PEER_HALF), :].astype(BF16)
        s1_all = jnp.dot(k1_ref[...], q1, preferred_element_type=F32)
        s2_all = jnp.dot(k2_ref[...], q2, preferred_element_type=F32)
        for lt in range(MID_TILE // lanes):
            sl = slice(lt * lanes, (lt + 1) * lanes)
            s1 = s1_all[:, sl]
            s2 = s2_all[:, sl]
            v1r = _top16_of_keys(s1)
            v2r = _top16_of_keys(s2)
            r2 = _rank16(s2, v2r)
            v1 = jnp.concatenate(v1r, axis=0)
            v2 = jnp.concatenate(v2r, axis=0)
            blocks = [v2 + v1r[0]]
            for a in range(1, 8):
                blocks.append(v2[0:8, :] + v1r[a])
            blocks.append(v1[8:16, :] + v2r[0])
            tops = _top16(jnp.concatenate(blocks, axis=0))
            m = tops[0]
            tau = tops[PEER_TOPK - 1]
            zsum = jnp.zeros_like(m)
            for row in tops:
                zsum = zsum + jnp.exp(row - m)
            theta = jnp.full(v2.shape, -NEG_BIG, F32)
            for a in range(PEER_TOPK):
                theta = jnp.minimum(theta, jnp.where(v1r[a] + v2 >= tau, v1r[a], -NEG_BIG))
            c1 = jnp.zeros(s1.shape, F32)
            for b in range(8):
                c1 = c1 + jnp.where(s1 >= theta[b:b + 1, :], 1.0, 0.0)
            n_hi = jnp.sum(jnp.where(theta[8:16, :] < 1.0e38, 1.0, 0.0), axis=0, keepdims=True)
            c1 = c1 + jnp.where(s1 >= v1r[0], n_hi, 0.0)
            c1_ref[h, :, sl] = c1
            e1_ref[h, :, sl] = jnp.exp(s1 - v1r[0]) * (1.0 / zsum)
            r2_ref[h, :, sl] = r2.astype(BF16)
            e2_ref[h, :, sl] = jnp.exp(s2 - v2r[0]).astype(BF16)
        return carry

    lax.fori_loop(0, PEER_HEADS, head_body, 0)


def _mid_call(z, hf, hb, gag, ybcd, mods, lw, layer, n_ctx_tok, seq, n_batch, skip_ctx):
    n, d = z.shape
    tt = MID_TILE
    nct = n_ctx_tok // tt
    tps = seq // tt
    tile_off = nct if skip_ctx else 0
    nt = n // tt - tile_off
    nq = PEER_HEADS * 2 * PEER_HALF

    def mrow(i):
        j = i + tile_off
        return jnp.where(j < nct, n_batch, (j - nct) // tps)

    full = lambda shape: pl.BlockSpec(shape, lambda i: (0,) * len(shape))
    tok = lambda w: pl.BlockSpec((tt, w), lambda i: (i + tile_off, 0))
    in_specs = [tok(d), tok(W_GROUP), tok(W_GROUP), tok(W_GROUP), tok(3 * W_GROUP),
                pl.BlockSpec((None, N_MOD, d), lambda i: (mrow(i), 0, 0)),
                pl.BlockSpec((None, d, d), lambda i: (layer, 0, 0)), full((1, d)),
                pl.BlockSpec((None, nq, d), lambda i: (layer, 0, 0)),
                full((N_KEYS, PEER_HALF)), full((N_KEYS, PEER_HALF))]
    hk = lambda: pl.BlockSpec((PEER_HEADS, N_KEYS, tt), lambda i: (0, 0, i + tile_off))
    out_specs = [tok(d), pl.BlockSpec((d, tt), lambda i: (0, i + tile_off)), hk(), hk(), hk(), hk()]
    hk_shape = lambda dt: jax.ShapeDtypeStruct((PEER_HEADS, N_KEYS, n), dt)
    out_shape = [jax.ShapeDtypeStruct((n, d), F32), jax.ShapeDtypeStruct((d, n), BF16),
                 hk_shape(F32), hk_shape(F32), hk_shape(BF16), hk_shape(BF16)]
    return pl.pallas_call(
        _mid_kernel,
        grid=(nt,),
        in_specs=in_specs, out_specs=out_specs, out_shape=out_shape,
        scratch_shapes=[pltpu.VMEM((nq, tt), F32)],
        compiler_params=pltpu.CompilerParams(vmem_limit_bytes=VMEM_LIMIT),
        name="mid",
    )(z, hf, hb, gag, ybcd, mods, lw["w_out"], lw["g_ffn"], lw["wq_t"], lw["k1"], lw["k2"])


def _peer_kernel(final, ft_ref, c1_ref, e1_ref, r2_ref, e2_ref, u_ref, vt_ref, z_ref, mod_ref,
                 gfin_ref, zo_ref, acc_ref, a_scr, h_scr0, h_scr1, w_scr0, w_scr1):
    c = pl.program_id(1)
    tl = PEER_LANES
    sub = PEER_SUB
    pk = BF16_ROWS
    n_sub = PEER_ECHUNK // sub
    h_scr = (h_scr0, h_scr1)
    w_scr = (w_scr0, w_scr1)

    @pl.when(c == 0)
    def _():
        acc_ref[...] = jnp.zeros_like(acc_ref)

    def pre_act(k):
        h_scr[k % 2][...] = jnp.dot(u_ref[k * sub:(k + 1) * sub, :], ft_ref[...],
                                    preferred_element_type=F32)

    def gates(k):
        for il in range(sub // N_KEYS):
            i_loc = k * (sub // N_KEYS) + il
            for lt in range(PEER_TILE // tl):
                ls = slice(lt * tl, (lt + 1) * tl)
                w = jnp.zeros((N_KEYS // pk, pk, tl), BF16)
                for h in range(PEER_HEADS):
                    c1 = jnp.broadcast_to(c1_ref[h, i_loc:i_loc + 1, ls], (pk, tl)).astype(BF16)
                    e1 = jnp.broadcast_to(e1_ref[h, i_loc:i_loc + 1, ls], (pk, tl)).astype(BF16)
                    sel = jnp.where(r2_ref[h, :, :, ls] < c1[None], e2_ref[h, :, :, ls], 0.0)
                    w = w + sel * e1[None]
                w_scr[k % 2][il * N_KEYS:(il + 1) * N_KEYS, ls] = w.reshape(N_KEYS, tl)

    def activate(k):
        g = _gelu_sigmoid(h_scr[k % 2][...].astype(BF16))
        a_scr[k * sub:(k + 1) * sub, :] = g * w_scr[k % 2][...]

    def project(k):
        acc_ref[...] += jnp.dot(vt_ref[:, k * sub:(k + 1) * sub], a_scr[k * sub:(k + 1) * sub, :],
                                preferred_element_type=F32)

    pre_act(0)
    gates(0)
    for k in range(n_sub):
        if k + 1 < n_sub:
            pre_act(k + 1)
        if k >= 1:
            project(k - 1)
        activate(k)
        if k + 1 < n_sub:
            gates(k + 1)
    project(n_sub - 1)

    @pl.when(c == pl.num_programs(1) - 1)
    def _():
        z = z_ref[...] + mod_ref[5:6, :] * acc_ref[...].T
        zo_ref[...] = _rms(z, gfin_ref[...]) if final else z


def _peer_call(z, ft, c1, e1, r2, e2, mods, lw, layer, g_final, n_ctx_tok, seq, n_batch, final):
    n, d = z.shape
    t = PEER_TILE
    ec = PEER_ECHUNK
    n_exp = lw["u"].shape[1]
    nct = n_ctx_tok // t
    tps = seq // t
    ic = ec // N_KEYS
    pk = BF16_ROWS
    off = nct if final else 0
    r2 = r2.reshape(PEER_HEADS, N_KEYS // pk, pk, n)
    e2 = e2.reshape(PEER_HEADS, N_KEYS // pk, pk, n)

    def mrow(i):
        j = i + off
        return jnp.where(j < nct, n_batch, (j - nct) // tps)

    in_specs = [
        pl.BlockSpec((d, t), lambda i, c: (0, i + off)),
        pl.BlockSpec((PEER_HEADS, ic, t), lambda i, c: (0, c, i + off)),
        pl.BlockSpec((PEER_HEADS, ic, t), lambda i, c: (0, c, i + off)),
        pl.BlockSpec((PEER_HEADS, N_KEYS // pk, pk, t), lambda i, c: (0, 0, 0, i + off)),
        pl.BlockSpec((PEER_HEADS, N_KEYS // pk, pk, t), lambda i, c: (0, 0, 0, i + off)),
        pl.BlockSpec((None, ec, d), lambda i, c: (layer, c, 0)),
        pl.BlockSpec((None, d, ec), lambda i, c: (layer, 0, c)),
        pl.BlockSpec((t, d), lambda i, c: (i + off, 0)),
        pl.BlockSpec((None, N_MOD, d), lambda i, c: (mrow(i), 0, 0)),
        pl.BlockSpec((1, d), lambda i, c: (0, 0)),
    ]
    return pl.pallas_call(
        functools.partial(_peer_kernel, final),
        grid=(n // t - off, n_exp // ec),
        in_specs=in_specs,
        out_specs=pl.BlockSpec((t, d), lambda i, c: (i, 0)),
        out_shape=jax.ShapeDtypeStruct((n - off * t, d), F32),
        scratch_shapes=[pltpu.VMEM((d, t), F32), pltpu.VMEM((ec, t), BF16)]
        + [pltpu.VMEM((PEER_SUB, t), F32)] * 2 + [pltpu.VMEM((PEER_SUB, t), BF16)] * 2,
        compiler_params=pltpu.CompilerParams(vmem_limit_bytes=VMEM_LIMIT),
        name="peer",
    )(ft, c1, e1, r2, e2, lw["u"], lw["v_t"], z, mods, g_final)


def _block_diag(w):
    h, a, b = w.shape
    eye = jnp.eye(h, dtype=w.dtype)
    return (eye[:, None, :, None] * w[:, :, None, :]).reshape(h * a, h * b)


def _grid_sincos(n, dim):
    rows = n // GRID_W
    row = jnp.repeat(jnp.arange(rows, dtype=F32), GRID_W)
    col = jnp.tile(jnp.arange(GRID_W, dtype=F32), rows)
    quarter = dim // 4
    omega = POS_BASE ** (-jnp.arange(quarter, dtype=F32) / quarter)

    def axis_emb(p):
        ang = p[:, None] * omega[None, :]
        return jnp.concatenate([jnp.sin(ang), jnp.cos(ang)], axis=-1)

    return jnp.concatenate([axis_emb(row), axis_emb(col)], axis=-1)


def kernel(x, c, ctx, c_ctx, w_ada, b_ada, g_mix, g_ffn, w_in, w_out, conv_a_w, conv_a_b, lru_l, lru_wr, lru_br, lru_wi, lru_bi, pool_w, pool_scale, conv_c_w, sgu_g, sgu_w, sgu_b, peer_wq, peer_k1, peer_k2, peer_u, peer_v, g_final):
    bsz, seq, d = x.shape
    ctx_len = ctx.shape[1]
    depth = w_ada.shape[0]
    assert ctx_len == TOK_TILE and seq % TOK_TILE == 0 and seq % PEER_TILE == 0 and seq % MID_TILE == 0
    assert (bsz * ctx_len) % PEER_TILE == 0 and (bsz * ctx_len) % MID_TILE == 0 and bsz + 1 <= 8
    n_ctx_tok = bsz * ctx_len
    nct = n_ctx_tok // TOK_TILE
    tps = seq // TOK_TILE

    cvec = jnp.concatenate([c, c_ctx[None, :], jnp.zeros((8 - bsz - 1, d), F32)], axis=0)
    mods_all = _ada_call(cvec, w_ada, b_ada).reshape(depth, 8, N_MOD, d)

    pos_tab = jnp.concatenate([jnp.zeros((TOK_TILE, d), F32), _grid_sincos(seq, d)], axis=0)
    z = jnp.concatenate([ctx.reshape(n_ctx_tok, d), x.reshape(bsz * seq, d)], axis=0)

    stacked = {
        "w_in": w_in.astype(BF16), "w_out": w_out.astype(BF16),
        "wq_t": jnp.swapaxes(peer_wq, 1, 2).astype(BF16),
        "u": peer_u.astype(BF16), "v_t": jnp.swapaxes(peer_v, 1, 2).astype(BF16),
    }

    for l in range(depth):
        wg = jnp.stack([jnp.concatenate([_block_diag(lru_wr[l, dd]), _block_diag(lru_wi[l, dd])], axis=1)
                        for dd in range(2)]).astype(BF16)
        bg = jnp.stack([jnp.concatenate([lru_br[l, dd].reshape(1, -1), lru_bi[l, dd].reshape(1, -1)], axis=1)
                        for dd in range(2)])
        lw = {
            "g_mix": g_mix[l].reshape(1, d), "g_ffn": g_ffn[l].reshape(1, d),
            "conv_a_w": conv_a_w[l], "conv_a_b": conv_a_b[l].reshape(1, -1),
            "conv_c_w": conv_c_w[l],
            "pool_w": _block_diag(pool_w[l]).astype(BF16), "pool_scale": pool_scale[l].reshape(1, -1),
            "sgu_g": sgu_g[l].reshape(1, -1),
            "sgu_w": jnp.transpose(sgu_w[l], (1, 0, 2)).reshape(SGU_CHUNK, SGU_HEADS * SGU_CHUNK).astype(BF16),
            "sgu_b": jnp.repeat(sgu_b[l].T, W_GROUP // SGU_HEADS, axis=1),
            "lru_wg": wg, "lru_bg": bg, "lru_l": lru_l[l],
            "k1": peer_k1[l].astype(BF16), "k2": peer_k2[l].astype(BF16),
        }
        lw.update(stacked)
        mods = mods_all[l]
        xa, gag, ybcd = _mix_in_call(z, pos_tab, mods, lw, l, nct, tps, bsz)
        hf, hb = _scan_call(xa, lw, nct, tps, bsz)
        last = l == depth - 1
        z, ft, c1, e1, r2, e2 = _mid_call(z, hf, hb, gag, ybcd, mods, lw, l, n_ctx_tok, seq, bsz, last)
        z = _peer_call(z, ft, c1, e1, r2, e2, mods, lw, l, g_final.reshape(1, d), n_ctx_tok, seq,
                       bsz, last)

    return z.reshape(bsz, seq, d)
```

```python
import functools

import jax
import jax.numpy as jnp
from jax import lax
from jax.experimental import pallas as pl
from jax.experimental.pallas import tpu as pltpu

F32 = jnp.float32
BF16 = jnp.bfloat16

W_GROUP = 256
LRU_HEADS = 4
LRU_C = 8.0
POOL_WINDOWS = (2, 4, 8, 16)
SGU_CHUNK = 128
SGU_HEADS = 4
PROJ_DIM = 8 * W_GROUP
N_KEYS = 128
PEER_HEADS = 8
PEER_HALF = 128
PEER_TOPK = 16
N_MOD = 6
EPS = 1e-6
POS_BASE = 10000.0
GRID_W = 64

TOK_TILE = 256
MID_TILE = 512
MID_HEADS_PER_ITER = 4
HALO = 8
PEER_TILE = 512
PEER_ECHUNK = 2048
PEER_SUB = 512
PEER_LANES = 256
BF16_ROWS = 16
NEG_BIG = -3.0e38
VMEM_LIMIT = 56 * 1024 * 1024


def _gelu(x):
    return 0.5 * x * (1.0 + jnp.tanh(0.7978845608028654 * (x + 0.044715 * x * x * x)))


def _gelu_sigmoid(x):
    k1 = -2.0 * 0.7978845608028654 * 1.4426950408889634
    return x / (1.0 + jnp.exp2(x * (x * x * (k1 * 0.044715) + k1)))


def _rms(x, g):
    return x * lax.rsqrt(jnp.mean(x * x, axis=-1, keepdims=True) + EPS) * g


def _ada_kernel(c_ref, w_ref, b_ref, o_ref):
    cv = c_ref[...]
    s = cv * jax.nn.sigmoid(cv)
    o_ref[...] = jnp.dot(s.astype(BF16), w_ref[...].astype(BF16),
                         preferred_element_type=F32) + b_ref[...]


def _ada_call(cvec, w_ada, b_ada):
    depth, d, nm = w_ada.shape
    tn = 1536
    return pl.pallas_call(
        _ada_kernel,
        grid=(depth, nm // tn),
        in_specs=[pl.BlockSpec((8, d), lambda l, j: (0, 0)),
                  pl.BlockSpec((None, d, tn), lambda l, j: (l, 0, j)),
                  pl.BlockSpec((None, 1, tn), lambda l, j: (l, 0, j))],
        out_specs=pl.BlockSpec((None, 8, tn), lambda l, j: (l, 0, j)),
        out_shape=jax.ShapeDtypeStruct((depth, 8, nm), F32),
        compiler_params=pltpu.CompilerParams(vmem_limit_bytes=VMEM_LIMIT),
        name="ada",
    )(cvec, w_ada, b_ada.reshape(depth, 1, nm))


def _mix_in_kernel(nct, tps, zp_ref, z_ref, zn_ref, pp_ref, p_ref, pn_ref, mod_ref, gmix_ref,
                   win_ref, cwa_ref, cba_ref, cwc_ref, poolw_ref, pscale_ref, sgug_ref,
                   sguw_ref, sgub_ref, xa_ref, gag_ref, ybcd_ref,
                   ext_scr, sa_scr, sb_scr, sc_scr, sd_scr):
    tt = TOK_TILE
    i = pl.program_id(0)
    is_ctx = i < nct
    pos_tile = jnp.where(is_ctx, 0, (i - nct) % tps)
    seq_tiles = jnp.where(is_ctx, 1, tps)
    first = pos_tile == 0
    last = pos_tile == seq_tiles - 1
    t0 = pos_tile * tt
    n_seq = seq_tiles * tt

    z_ext = jnp.concatenate([zp_ref[...], z_ref[...], zn_ref[...]], axis=0)
    pos_ext = jnp.concatenate([pp_ref[...], p_ref[...], pn_ref[...]], axis=0)
    a = _rms(z_ext, gmix_ref[...])
    a = a * (1.0 + mod_ref[1:2, :]) + mod_ref[0:1, :] + pos_ext
    row = lax.broadcasted_iota(jnp.int32, (tt + 2 * HALO, 1), 0)
    keep_prev = jnp.where(first, 0.0, 1.0)
    keep_next = jnp.where(last, 0.0, 1.0)
    keep = jnp.where(row < HALO, keep_prev, jnp.where(row >= tt + HALO, keep_next, 1.0))
    a = a * keep
    p = jnp.dot(a.astype(BF16), win_ref[...], preferred_element_type=F32)

    zeros8 = jnp.zeros((HALO, PROJ_DIM), F32)
    ext_scr[0:HALO, :] = zeros8
    ext_scr[tt + 3 * HALO:tt + 4 * HALO, :] = zeros8
    ext_scr[HALO:tt + 3 * HALO, :] = p
    m0 = 2 * HALO
    ne = tt + 2 * HALO

    cwa = cwa_ref[...]
    xa = cba_ref[...]
    for k in range(4):
        xa = xa + cwa[k:k + 1, :] * ext_scr[pl.ds(m0 - 2 + k, tt), 0:W_GROUP]
    xa_ref[...] = xa
    gag_ref[...] = _gelu(ext_scr[pl.ds(m0, tt), W_GROUP:2 * W_GROUP])

    z8 = jnp.zeros((HALO, W_GROUP), F32)
    for scr in (sa_scr, sb_scr, sc_scr, sd_scr):
        scr[0:HALO, :] = z8
        scr[tt + 3 * HALO:tt + 4 * HALO, :] = z8
    c0 = 2 * W_GROUP
    sa_scr[HALO:HALO + ne, :] = (ext_scr[pl.ds(HALO - 1, ne), c0:c0 + W_GROUP]
                                 + ext_scr[pl.ds(HALO, ne), c0:c0 + W_GROUP])
    sb_scr[HALO:HALO + ne, :] = sa_scr[pl.ds(HALO - 1, ne), :] + sa_scr[pl.ds(HALO + 1, ne), :]
    sc_scr[HALO:HALO + ne, :] = sb_scr[pl.ds(HALO - 2, ne), :] + sb_scr[pl.ds(HALO + 2, ne), :]
    p16 = sc_scr[pl.ds(m0 - 4, tt), :] + sc_scr[pl.ds(m0 + 4, tt), :]
    p2 = sa_scr[pl.ds(m0, tt), :]
    p4 = sb_scr[pl.ds(m0, tt), :]
    p8 = sc_scr[pl.ds(m0, tt), :]
    grp = lax.broadcasted_iota(jnp.int32, (tt, W_GROUP), 1) // (W_GROUP // len(POOL_WINDOWS))
    tpos = t0 + lax.broadcasted_iota(jnp.int32, (tt, W_GROUP), 0)
    sums = jnp.where(grp == 0, p2, jnp.where(grp == 1, p4, jnp.where(grp == 2, p8, p16)))
    half = jnp.where(grp == 0, 1, jnp.where(grp == 1, 2, jnp.where(grp == 2, 4, 8)))
    cnt = jnp.minimum(tpos + half, n_seq) - jnp.maximum(tpos - half, 0)
    bx = ext_scr[pl.ds(m0, tt), c0:c0 + W_GROUP]
    dpool = sums / cnt.astype(F32) - bx
    y_b = jnp.dot(dpool.astype(BF16), poolw_ref[...], preferred_element_type=F32) * pscale_ref[...]
    ybcd_ref[:, 0:W_GROUP] = y_b.astype(BF16)

    c_cb, c_cc, c_ch = 3 * W_GROUP, 4 * W_GROUP, 5 * W_GROUP
    sd_scr[HALO:HALO + ne, :] = (ext_scr[pl.ds(HALO, ne), c_cc:c_cc + W_GROUP]
                                 * ext_scr[pl.ds(HALO, ne), c_ch:c_ch + W_GROUP])
    cwc = cwc_ref[...]
    conv = jnp.zeros((tt, W_GROUP), F32)
    for k in range(3):
        conv = conv + cwc[k:k + 1, :] * sd_scr[pl.ds(m0 - 1 + k, tt), :]
    y_c = ext_scr[pl.ds(m0, tt), c_cb:c_cb + W_GROUP] * conv
    ybcd_ref[:, W_GROUP:2 * W_GROUP] = y_c.astype(BF16)

    c_du, c_dv = 6 * W_GROUP, 7 * W_GROUP
    u = _gelu(ext_scr[pl.ds(m0, tt), c_du:c_du + W_GROUP])
    gv = _gelu(ext_scr[pl.ds(m0, tt), c_dv:c_dv + W_GROUP])
    xc = gv - jnp.mean(gv, axis=-1, keepdims=True)
    v = xc * lax.rsqrt(jnp.mean(xc * xc, axis=-1, keepdims=True) + EPS) * sgug_ref[...]
    head = lax.broadcasted_iota(jnp.int32, (SGU_CHUNK, W_GROUP), 1) // (W_GROUP // SGU_HEADS)
    for c in range(tt // SGU_CHUNK):
        vc = v[c * SGU_CHUNK:(c + 1) * SGU_CHUNK, :]
        stack = jnp.concatenate([jnp.where(head == h, vc, 0.0) for h in range(SGU_HEADS)], axis=0)
        mixed = jnp.dot(sguw_ref[...], stack.astype(BF16), preferred_element_type=F32) + sgub_ref[...]
        y_d = u[c * SGU_CHUNK:(c + 1) * SGU_CHUNK, :] * mixed
        ybcd_ref[c * SGU_CHUNK:(c + 1) * SGU_CHUNK, 2 * W_GROUP:3 * W_GROUP] = y_d.astype(BF16)


def _mix_in_call(z, pos_tab, mods, lw, layer, nct, tps, n_batch):
    n, d = z.shape
    tt = TOK_TILE
    nt = n // tt
    hb = tt // HALO
    n_pos = pos_tab.shape[0]

    def mrow(i):
        return jnp.where(i < nct, n_batch, (i - nct) // tps)

    def pblk(i):
        return jnp.where(i < nct, 0, 1 + (i - nct) % tps)

    full = lambda shape: pl.BlockSpec(shape, lambda i: (0,) * len(shape))
    in_specs = [
        pl.BlockSpec((HALO, d), lambda i: (jnp.maximum(i * hb - 1, 0), 0)),
        pl.BlockSpec((tt, d), lambda i: (i, 0)),
        pl.BlockSpec((HALO, d), lambda i: (jnp.minimum((i + 1) * hb, n // HALO - 1), 0)),
        pl.BlockSpec((HALO, d), lambda i: (jnp.maximum(pblk(i) * hb - 1, 0), 0)),
        pl.BlockSpec((tt, d), lambda i: (pblk(i), 0)),
        pl.BlockSpec((HALO, d), lambda i: (jnp.minimum((pblk(i) + 1) * hb, n_pos // HALO - 1), 0)),
        pl.BlockSpec((None, N_MOD, d), lambda i: (mrow(i), 0, 0)),
        full((1, d)),
        pl.BlockSpec((None, d, PROJ_DIM), lambda i: (layer, 0, 0)),
        full((4, W_GROUP)), full((1, W_GROUP)), full((3, W_GROUP)),
        full((W_GROUP, W_GROUP)), full((1, W_GROUP)), full((1, W_GROUP)),
        full((SGU_CHUNK, SGU_HEADS * SGU_CHUNK)), full((SGU_CHUNK, W_GROUP)),
    ]
    out_specs = [pl.BlockSpec((tt, W_GROUP), lambda i: (i, 0)),
                 pl.BlockSpec((tt, W_GROUP), lambda i: (i, 0)),
                 pl.BlockSpec((tt, 3 * W_GROUP), lambda i: (i, 0))]
    out_shape = [jax.ShapeDtypeStruct((n, W_GROUP), F32),
                 jax.ShapeDtypeStruct((n, W_GROUP), F32),
                 jax.ShapeDtypeStruct((n, 3 * W_GROUP), BF16)]
    er = tt + 4 * HALO
    return pl.pallas_call(
        functools.partial(_mix_in_kernel, nct, tps),
        grid=(nt,),
        in_specs=in_specs, out_specs=out_specs, out_shape=out_shape,
        scratch_shapes=[pltpu.VMEM((er, PROJ_DIM), F32)] + [pltpu.VMEM((er, W_GROUP), F32)] * 4,
        compiler_params=pltpu.CompilerParams(vmem_limit_bytes=VMEM_LIMIT),
        name="mix_in",
    )(z, z, z, pos_tab, pos_tab, pos_tab, mods, lw["g_mix"], lw["w_in"], lw["conv_a_w"],
      lw["conv_a_b"], lw["conv_c_w"], lw["pool_w"], lw["pool_scale"], lw["sgu_g"],
      lw["sgu_w"], lw["sgu_b"])


def _chunk_scan(a, b, reverse):
    t = a.shape[0]
    row = lax.broadcasted_iota(jnp.int32, a.shape, 0)
    d = 1
    while d < t:
        shift = t - d if reverse else d
        ok = (row < t - d) if reverse else (row >= d)
        a_s = jnp.where(ok, pltpu.roll(a, shift, axis=0), 1.0)
        b_s = jnp.where(ok, pltpu.roll(b, shift, axis=0), 0.0)
        b = b + a * b_s
        a = a * a_s
        d *= 2
    return a, b


def _scan_kernel(xf_ref, xb_ref, wg_ref, bg_ref, lam_ref, hf_ref, hb_ref, carry_ref):
    tt = TOK_TILE
    s = pl.program_id(1)

    @pl.when(s == 0)
    def _():
        carry_ref[...] = jnp.zeros_like(carry_ref)

    for d, (x_ref, o_ref) in enumerate(((xf_ref, hf_ref), (xb_ref, hb_ref))):
        xa = x_ref[...]
        g = jnp.dot(xa.astype(BF16), wg_ref[d], preferred_element_type=F32) + bg_ref[d]
        r = jax.nn.sigmoid(g[:, 0:W_GROUP])
        gi = jax.nn.sigmoid(g[:, W_GROUP:2 * W_GROUP])
        neg_lam = -lam_ref[d:d + 1, :]
        softplus = jnp.maximum(neg_lam, 0.0) + jnp.log1p(jnp.exp(-jnp.abs(neg_lam)))
        log_a = (-LRU_C) * r * softplus
        a = jnp.exp(log_a)
        b = jnp.sqrt(1.0 - a * a) * (gi * xa)
        a_cum, h0 = _chunk_scan(a, b, reverse=(d == 1))
        h = h0 + a_cum * carry_ref[d:d + 1, :]
        o_ref[...] = h
        edge = 0 if d == 1 else tt - 1
        carry_ref[d:d + 1, :] = h[edge:edge + 1, :]


def _scan_call(xa, lw, nct, tps, n_batch):
    n, c = xa.shape
    tt = TOK_TILE
    fwd = lambda b, s: (jnp.where(s == 0, b, nct + b * tps + s - 1), 0)
    bwd = lambda b, s: (jnp.where(s == 0, b, nct + b * tps + tps - s), 0)
    full = lambda shape: pl.BlockSpec(shape, lambda b, s: (0,) * len(shape))
    return pl.pallas_call(
        _scan_kernel,
        grid=(n_batch, tps + 1),
        in_specs=[pl.BlockSpec((tt, c), fwd), pl.BlockSpec((tt, c), bwd),
                  full((2, c, 2 * c)), full((2, 1, 2 * c)), full((2, c))],
        out_specs=[pl.BlockSpec((tt, c), fwd), pl.BlockSpec((tt, c), bwd)],
        out_shape=[jax.ShapeDtypeStruct((n, c), F32)] * 2,
        scratch_shapes=[pltpu.VMEM((2, c), F32)],
        compiler_params=pltpu.CompilerParams(vmem_limit_bytes=VMEM_LIMIT),
        name="lru_scan",
    )(xa, xa, lw["lru_wg"], lw["lru_bg"], lw["lru_l"])


def _sort16_network():
    def merge(lo, hi, r):
        step = r * 2
        if step < hi - lo:
            yield from merge(lo, hi, step)
            yield from merge(lo + r, hi, step)
            yield from [(i, i + r) for i in range(lo + r, hi - r, step)]
        else:
            yield (lo, lo + r)

    def sort(lo, hi):
        if hi - lo >= 1:
            mid = lo + (hi - lo) // 2
            yield from sort(lo, mid)
            yield from sort(mid + 1, hi)
            yield from merge(lo, hi, 1)

    return tuple(sort(0, PEER_TOPK - 1))


_SORT16 = _sort16_network()


def _top16_of_columns(cols):
    cols = list(cols)
    rows = []
    for r in range(PEER_TOPK):
        mx = jnp.max(cols[0], axis=0, keepdims=True)
        rows.append(mx)
        if r + 1 < PEER_TOPK:
            hit = cols[0] == mx
            depth = PEER_TOPK - r
            for d in range(depth - 1):
                cols[d] = jnp.where(hit, cols[d + 1], cols[d])
            cols[depth - 1] = jnp.where(hit, NEG_BIG, cols[depth - 1])
    return rows


def _top16_of_keys(s):
    cols = [s[8 * g:8 * g + 8, :] for g in range(N_KEYS // 8)]
    for a, b in _SORT16:
        cols[a], cols[b] = jnp.maximum(cols[a], cols[b]), jnp.minimum(cols[a], cols[b])
    return _top16_of_columns(cols)


def _rank16(s, v):
    m8 = s >= v[7]
    m4 = s >= jnp.where(m8, v[3], v[11])
    m2 = s >= jnp.where(m8, jnp.where(m4, v[1], v[5]), jnp.where(m4, v[9], v[13]))
    m1 = s >= jnp.where(m8, jnp.where(m4, jnp.where(m2, v[0], v[2]), jnp.where(m2, v[4], v[6])),
                        jnp.where(m4, jnp.where(m2, v[8], v[10]), jnp.where(m2, v[12], v[14])))
    rank = (jnp.where(m8, 0.0, 8.0) + jnp.where(m4, 0.0, 4.0)
            + jnp.where(m2, 0.0, 2.0) + jnp.where(m1, 0.0, 1.0))
    return jnp.where(s >= v[15], rank, float(PEER_TOPK))


def _mid_kernel(z_ref, hf_ref, hb_ref, gag_ref, ybcd_ref, mod_ref, wout_ref, gffn_ref, wqt_ref,
                k1_ref, k2_ref, zo_ref, ft_ref, c1_ref, e1_ref, r2_ref, e2_ref, qt_scr):
    ya = ((hf_ref[...] + hb_ref[...]) * gag_ref[...]).astype(BF16)
    y = jnp.concatenate([ya, ybcd_ref[...]], axis=1)
    o = jnp.dot(y, wout_ref[...], preferred_element_type=F32)
    z = z_ref[...] + mod_ref[2:3, :] * o
    zo_ref[...] = z
    f = _rms(z, gffn_ref[...]) * (1.0 + mod_ref[4:5, :]) + mod_ref[3:4, :]
    ft = f.T.astype(BF16)
    ft_ref[...] = ft
    qt_scr[...] = jnp.dot(wqt_ref[...], ft, preferred_element_type=F32)
    lanes = 128

    def head_body(h, carry):
        base = pl.multiple_of(h * (2 * PEER_HALF), 2 * PEER_HALF)
        q1 = qt_scr[pl.ds(base, PEER_HALF), :].astype(BF16)
        q2 = qt_scr[pl.ds(base + PEER_HALF, PEER_HALF), :].astype(BF16)
        s1_all = jnp.dot(k1_ref[...], q1, preferred_element_type=F32)
        s2_all = jnp.dot(k2_ref[...], q2, preferred_element_type=F32)
        for lt in range(MID_TILE // lanes):
            sl = slice(lt * lanes, (lt + 1) * lanes)
            s1 = s1_all[:, sl]
            s2 = s2_all[:, sl]
            v1r = _top16_of_keys(s1)
            v2r = _top16_of_keys(s2)
            r2 = _rank16(s2, v2r)
            v1 = jnp.concatenate(v1r, axis=0)
            v2 = jnp.concatenate(v2r, axis=0)
            tops = _top16_of_columns([v1 + v2r[b] for b in range(PEER_TOPK)])
            m = tops[0]
            tau = tops[PEER_TOPK - 1]
            zsum = jnp.zeros_like(m)
            for row in tops:
                zsum = zsum + jnp.exp(row - m)
            theta = jnp.full(v2.shape, -NEG_BIG, F32)
            for a in range(PEER_TOPK):
                theta = jnp.minimum(theta, jnp.where(v1r[a] + v2 >= tau, v1r[a], -NEG_BIG))
            th = [theta[b:b + 1, :] for b in range(8)]
            m4 = s1 >= th[3]
            m2 = s1 >= jnp.where(m4, th[5], th[1])
            m1 = s1 >= jnp.where(m4, jnp.where(m2, th[6], th[4]), jnp.where(m2, th[2], th[0]))
            c1 = jnp.where(m4, 4.0, 0.0) + jnp.where(m2, 2.0, 0.0) + jnp.where(m1, 1.0, 0.0)
            c1 = jnp.where(s1 >= th[7], 8.0, c1)
            n_hi = jnp.sum(jnp.where(theta[8:16, :] < 1.0e38, 1.0, 0.0), axis=0, keepdims=True)
            c1 = c1 + jnp.where(s1 >= v1r[0], n_hi, 0.0)
            c1_ref[h, :, sl] = c1
            e1_ref[h, :, sl] = jnp.exp(s1 - v1r[0]) * (1.0 / zsum)
            r2_ref[h, :, sl] = r2.astype(BF16)
            e2_ref[h, :, sl] = jnp.exp(s2 - v2r[0]).astype(BF16)
        return carry

    def head_group(hg, carry):
        for j in range(MID_HEADS_PER_ITER):
            head_body(MID_HEADS_PER_ITER * hg + j, carry)
        return carry

    lax.fori_loop(0, PEER_HEADS // MID_HEADS_PER_ITER, head_group, 0)


def _mid_call(z, hf, hb, gag, ybcd, mods, lw, layer, n_ctx_tok, seq, n_batch, skip_ctx):
    n, d = z.shape
    tt = MID_TILE
    nct = n_ctx_tok // tt
    tps = seq // tt
    tile_off = nct if skip_ctx else 0
    nt = n // tt - tile_off
    nq = PEER_HEADS * 2 * PEER_HALF

    def mrow(i):
        j = i + tile_off
        return jnp.where(j < nct, n_batch, (j - nct) // tps)

    full = lambda shape: pl.BlockSpec(shape, lambda i: (0,) * len(shape))
    tok = lambda w: pl.BlockSpec((tt, w), lambda i: (i + tile_off, 0))
    in_specs = [tok(d), tok(W_GROUP), tok(W_GROUP), tok(W_GROUP), tok(3 * W_GROUP),
                pl.BlockSpec((None, N_MOD, d), lambda i: (mrow(i), 0, 0)),
                pl.BlockSpec((None, d, d), lambda i: (layer, 0, 0)), full((1, d)),
                pl.BlockSpec((None, nq, d), lambda i: (layer, 0, 0)),
                full((N_KEYS, PEER_HALF)), full((N_KEYS, PEER_HALF))]
    hk = lambda: pl.BlockSpec((PEER_HEADS, N_KEYS, tt), lambda i: (0, 0, i + tile_off))
    out_specs = [tok(d), pl.BlockSpec((d, tt), lambda i: (0, i + tile_off)), hk(), hk(), hk(), hk()]
    hk_shape = lambda dt: jax.ShapeDtypeStruct((PEER_HEADS, N_KEYS, n), dt)
    out_shape = [jax.ShapeDtypeStruct((n, d), F32), jax.ShapeDtypeStruct((d, n), BF16),
                 hk_shape(F32), hk_shape(F32), hk_shape(BF16), hk_shape(BF16)]
    return pl.pallas_call(
        _mid_kernel,
        grid=(nt,),
        in_specs=in_specs, out_specs=out_specs, out_shape=out_shape,
        scratch_shapes=[pltpu.VMEM((nq, tt), F32)],
        compiler_params=pltpu.CompilerParams(vmem_limit_bytes=VMEM_LIMIT),
        name="mid",
    )(z, hf, hb, gag, ybcd, mods, lw["w_out"], lw["g_ffn"], lw["wq_t"], lw["k1"], lw["k2"])


def _peer_kernel(final, ft_ref, c1_ref, e1_ref, r2_ref, e2_ref, u_ref, vt_ref, z_ref, mod_ref,
                 gfin_ref, zo_ref, acc_ref, a_scr, h_scr0, h_scr1, w_scr0, w_scr1):
    c = pl.program_id(1)
    tl = PEER_LANES
    sub = PEER_SUB
    pk = BF16_ROWS
    n_sub = PEER_ECHUNK // sub
    h_scr = (h_scr0, h_scr1)
    w_scr = (w_scr0, w_scr1)

    @pl.when(c == 0)
    def _():
        acc_ref[...] = jnp.zeros_like(acc_ref)

    def pre_act(k):
        h_scr[k % 2][...] = jnp.dot(u_ref[k * sub:(k + 1) * sub, :], ft_ref[...],
                                    preferred_element_type=F32)

    def gates(k):
        for il in range(sub // N_KEYS):
            i_loc = k * (sub // N_KEYS) + il
            for lt in range(PEER_TILE // tl):
                ls = slice(lt * tl, (lt + 1) * tl)
                w = jnp.zeros((N_KEYS // pk, pk, tl), BF16)
                for h in range(PEER_HEADS):
                    c1 = jnp.broadcast_to(c1_ref[h, i_loc:i_loc + 1, ls], (pk, tl)).astype(BF16)
                    e1 = jnp.broadcast_to(e1_ref[h, i_loc:i_loc + 1, ls], (pk, tl)).astype(BF16)
                    sel = jnp.where(r2_ref[h, :, :, ls] < c1[None], e2_ref[h, :, :, ls], 0.0)
                    w = w + sel * e1[None]
                w_scr[k % 2][il * N_KEYS:(il + 1) * N_KEYS, ls] = w.reshape(N_KEYS, tl)

    def activate(k):
        g = _gelu_sigmoid(h_scr[k % 2][...].astype(BF16))
        a_scr[k * sub:(k + 1) * sub, :] = g * w_scr[k % 2][...]

    def project(k):
        acc_ref[...] += jnp.dot(vt_ref[:, k * sub:(k + 1) * sub], a_scr[k * sub:(k + 1) * sub, :],
                                preferred_element_type=F32)

    pre_act(0)
    gates(0)
    for k in range(n_sub):
        if k + 1 < n_sub:
            pre_act(k + 1)
        if k >= 1:
            project(k - 1)
        activate(k)
        if k + 1 < n_sub:
            gates(k + 1)
    project(n_sub - 1)

    @pl.when(c == pl.num_programs(1) - 1)
    def _():
        z = z_ref[...] + mod_ref[5:6, :] * acc_ref[...].T
        zo_ref[...] = _rms(z, gfin_ref[...]) if final else z


def _peer_call(z, ft, c1, e1, r2, e2, mods, lw, layer, g_final, n_ctx_tok, seq, n_batch, final):
    n, d = z.shape
    t = PEER_TILE
    ec = PEER_ECHUNK
    n_exp = lw["u"].shape[1]
    nct = n_ctx_tok // t
    tps = seq // t
    ic = ec // N_KEYS
    pk = BF16_ROWS
    off = nct if final else 0
    r2 = r2.reshape(PEER_HEADS, N_KEYS // pk, pk, n)
    e2 = e2.reshape(PEER_HEADS, N_KEYS // pk, pk, n)

    def mrow(i):
        j = i + off
        return jnp.where(j < nct, n_batch, (j - nct) // tps)

    in_specs = [
        pl.BlockSpec((d, t), lambda i, c: (0, i + off)),
        pl.BlockSpec((PEER_HEADS, ic, t), lambda i, c: (0, c, i + off)),
        pl.BlockSpec((PEER_HEADS, ic, t), lambda i, c: (0, c, i + off)),
        pl.BlockSpec((PEER_HEADS, N_KEYS // pk, pk, t), lambda i, c: (0, 0, 0, i + off)),
        pl.BlockSpec((PEER_HEADS, N_KEYS // pk, pk, t), lambda i, c: (0, 0, 0, i + off)),
        pl.BlockSpec((None, ec, d), lambda i, c: (layer, c, 0)),
        pl.BlockSpec((None, d, ec), lambda i, c: (layer, 0, c)),
        pl.BlockSpec((t, d), lambda i, c: (i + off, 0)),
        pl.BlockSpec((None, N_MOD, d), lambda i, c: (mrow(i), 0, 0)),
        pl.BlockSpec((1, d), lambda i, c: (0, 0)),
    ]
    return pl.pallas_call(
        functools.partial(_peer_kernel, final),
        grid=(n // t - off, n_exp // ec),
        in_specs=in_specs,
        out_specs=pl.BlockSpec((t, d), lambda i, c: (i, 0)),
        out_shape=jax.ShapeDtypeStruct((n - off * t, d), F32),
        scratch_shapes=[pltpu.VMEM((d, t), F32), pltpu.VMEM((ec, t), BF16)]
        + [pltpu.VMEM((PEER_SUB, t), F32)] * 2 + [pltpu.VMEM((PEER_SUB, t), BF16)] * 2,
        compiler_params=pltpu.CompilerParams(vmem_limit_bytes=VMEM_LIMIT),
        name="peer",
    )(ft, c1, e1, r2, e2, lw["u"], lw["v_t"], z, mods, g_final)


def _block_diag(w):
    h, a, b = w.shape
    eye = jnp.eye(h, dtype=w.dtype)
    return (eye[:, None, :, None] * w[:, :, None, :]).reshape(h * a, h * b)


def _grid_sincos(n, dim):
    rows = n // GRID_W
    row = jnp.repeat(jnp.arange(rows, dtype=F32), GRID_W)
    col = jnp.tile(jnp.arange(GRID_W, dtype=F32), rows)
    quarter = dim // 4
    omega = POS_BASE ** (-jnp.arange(quarter, dtype=F32) / quarter)

    def axis_emb(p):
        ang = p[:, None] * omega[None, :]
        return jnp.concatenate([jnp.sin(ang), jnp.cos(ang)], axis=-1)

    return jnp.concatenate([axis_emb(row), axis_emb(col)], axis=-1)


def kernel(x, c, ctx, c_ctx, w_ada, b_ada, g_mix, g_ffn, w_in, w_out, conv_a_w, conv_a_b, lru_l, lru_wr, lru_br, lru_wi, lru_bi, pool_w, pool_scale, conv_c_w, sgu_g, sgu_w, sgu_b, peer_wq, peer_k1, peer_k2, peer_u, peer_v, g_final):
    bsz, seq, d = x.shape
    ctx_len = ctx.shape[1]
    depth = w_ada.shape[0]
    assert ctx_len == TOK_TILE and seq % TOK_TILE == 0 and seq % PEER_TILE == 0 and seq % MID_TILE == 0
    assert (bsz * ctx_len) % PEER_TILE == 0 and (bsz * ctx_len) % MID_TILE == 0 and bsz + 1 <= 8
    n_ctx_tok = bsz * ctx_len
    nct = n_ctx_tok // TOK_TILE
    tps = seq // TOK_TILE

    cvec = jnp.concatenate([c, c_ctx[None, :], jnp.zeros((8 - bsz - 1, d), F32)], axis=0)
    mods_all = _ada_call(cvec, w_ada, b_ada).reshape(depth, 8, N_MOD, d)

    pos_tab = jnp.concatenate([jnp.zeros((TOK_TILE, d), F32), _grid_sincos(seq, d)], axis=0)
    z = jnp.concatenate([ctx.reshape(n_ctx_tok, d), x.reshape(bsz * seq, d)], axis=0)

    stacked = {
        "w_in": w_in.astype(BF16), "w_out": w_out.astype(BF16),
        "wq_t": jnp.swapaxes(peer_wq, 1, 2).astype(BF16),
        "u": peer_u.astype(BF16), "v_t": jnp.swapaxes(peer_v, 1, 2).astype(BF16),
    }

    for l in range(depth):
        wg = jnp.stack([jnp.concatenate([_block_diag(lru_wr[l, dd]), _block_diag(lru_wi[l, dd])], axis=1)
                        for dd in range(2)]).astype(BF16)
        bg = jnp.stack([jnp.concatenate([lru_br[l, dd].reshape(1, -1), lru_bi[l, dd].reshape(1, -1)], axis=1)
                        for dd in range(2)])
        lw = {
            "g_mix": g_mix[l].reshape(1, d), "g_ffn": g_ffn[l].reshape(1, d),
            "conv_a_w": conv_a_w[l], "conv_a_b": conv_a_b[l].reshape(1, -1),
            "conv_c_w": conv_c_w[l],
            "pool_w": _block_diag(pool_w[l]).astype(BF16), "pool_scale": pool_scale[l].reshape(1, -1),
            "sgu_g": sgu_g[l].reshape(1, -1),
            "sgu_w": jnp.transpose(sgu_w[l], (1, 0, 2)).reshape(SGU_CHUNK, SGU_HEADS * SGU_CHUNK).astype(BF16),
            "sgu_b": jnp.repeat(sgu_b[l].T, W_GROUP // SGU_HEADS, axis=1),
            "lru_wg": wg, "lru_bg": bg, "lru_l": lru_l[l],
            "k1": peer_k1[l].astype(BF16), "k2": peer_k2[l].astype(BF16),
        }
        lw.update(stacked)
        mods = mods_all[l]
        xa, gag, ybcd = _mix_in_call(z, pos_tab, mods, lw, l, nct, tps, bsz)
        hf, hb = _scan_call(xa, lw, nct, tps, bsz)
        last = l == depth - 1
        z, ft, c1, e1, r2, e2 = _mid_call(z, hf, hb, gag, ybcd, mods, lw, l, n_ctx_tok, seq, bsz, last)
        z = _peer_call(z, ft, c1, e1, r2, e2, mods, lw, l, g_final.reshape(1, d), n_ctx_tok, seq,
                       bsz, last)

    return z.reshape(bsz, seq, d)
```

```python
import functools

import jax
import jax.numpy as jnp
from jax import lax
from jax.experimental import pallas as pl
from jax.experimental.pallas import tpu as pltpu

F32 = jnp.float32
BF16 = jnp.bfloat16

W_GROUP = 256
LRU_HEADS = 4
LRU_C = 8.0
POOL_WINDOWS = (2, 4, 8, 16)
SGU_CHUNK = 128
SGU_HEADS = 4
PROJ_DIM = 8 * W_GROUP
N_KEYS = 128
PEER_HEADS = 8
PEER_HALF = 128
PEER_TOPK = 16
N_MOD = 6
EPS = 1e-6
POS_BASE = 10000.0
GRID_W = 64

TOK_TILE = 256
MID_TILE = 512
MID_HEADS_PER_ITER = 4
HALO = 8
PEER_TILE = 512
PEER_ECHUNK = 2048
PEER_SUB = 512
PEER_LANES = 256
BF16_ROWS = 16
NEG_BIG = -3.0e38
VMEM_LIMIT = 56 * 1024 * 1024


def _gelu(x):
    return 0.5 * x * (1.0 + jnp.tanh(0.7978845608028654 * (x + 0.044715 * x * x * x)))


def _gelu_sigmoid(x):
    k1 = -2.0 * 0.7978845608028654 * 1.4426950408889634
    return x / (1.0 + jnp.exp2(x * (x * x * (k1 * 0.044715) + k1)))


def _rms(x, g):
    return x * lax.rsqrt(jnp.mean(x * x, axis=-1, keepdims=True) + EPS) * g


def _ada_kernel(c_ref, w_ref, b_ref, o_ref):
    cv = c_ref[...]
    s = cv * jax.nn.sigmoid(cv)
    o_ref[...] = jnp.dot(s.astype(BF16), w_ref[...].astype(BF16),
                         preferred_element_type=F32) + b_ref[...]


def _ada_call(cvec, w_ada, b_ada):
    depth, d, nm = w_ada.shape
    tn = 1536
    return pl.pallas_call(
        _ada_kernel,
        grid=(depth, nm // tn),
        in_specs=[pl.BlockSpec((8, d), lambda l, j: (0, 0)),
                  pl.BlockSpec((None, d, tn), lambda l, j: (l, 0, j)),
                  pl.BlockSpec((None, 1, tn), lambda l, j: (l, 0, j))],
        out_specs=pl.BlockSpec((None, 8, tn), lambda l, j: (l, 0, j)),
        out_shape=jax.ShapeDtypeStruct((depth, 8, nm), F32),
        compiler_params=pltpu.CompilerParams(vmem_limit_bytes=VMEM_LIMIT),
        name="ada",
    )(cvec, w_ada, b_ada.reshape(depth, 1, nm))


def _mix_in_kernel(nct, tps, zp_ref, z_ref, zn_ref, pp_ref, p_ref, pn_ref, mod_ref, gmix_ref,
                   win_ref, cwa_ref, cba_ref, cwc_ref, poolw_ref, pscale_ref, sgug_ref,
                   sguw_ref, sgub_ref, xa_ref, gag_ref, ybcd_ref,
                   ext_scr, sa_scr, sb_scr, sc_scr, sd_scr):
    tt = TOK_TILE
    i = pl.program_id(0)
    is_ctx = i < nct
    pos_tile = jnp.where(is_ctx, 0, (i - nct) % tps)
    seq_tiles = jnp.where(is_ctx, 1, tps)
    first = pos_tile == 0
    last = pos_tile == seq_tiles - 1
    t0 = pos_tile * tt
    n_seq = seq_tiles * tt

    z_ext = jnp.concatenate([zp_ref[...], z_ref[...], zn_ref[...]], axis=0)
    pos_ext = jnp.concatenate([pp_ref[...], p_ref[...], pn_ref[...]], axis=0)
    a = _rms(z_ext, gmix_ref[...])
    a = a * (1.0 + mod_ref[1:2, :]) + mod_ref[0:1, :] + pos_ext
    row = lax.broadcasted_iota(jnp.int32, (tt + 2 * HALO, 1), 0)
    keep_prev = jnp.where(first, 0.0, 1.0)
    keep_next = jnp.where(last, 0.0, 1.0)
    keep = jnp.where(row < HALO, keep_prev, jnp.where(row >= tt + HALO, keep_next, 1.0))
    a = a * keep
    p = jnp.dot(a.astype(BF16), win_ref[...], preferred_element_type=F32)

    zeros8 = jnp.zeros((HALO, PROJ_DIM), F32)
    ext_scr[0:HALO, :] = zeros8
    ext_scr[tt + 3 * HALO:tt + 4 * HALO, :] = zeros8
    ext_scr[HALO:tt + 3 * HALO, :] = p
    m0 = 2 * HALO
    ne = tt + 2 * HALO

    cwa = cwa_ref[...]
    xa = cba_ref[...]
    for k in range(4):
        xa = xa + cwa[k:k + 1, :] * ext_scr[pl.ds(m0 - 2 + k, tt), 0:W_GROUP]
    xa_ref[...] = xa
    gag_ref[...] = _gelu(ext_scr[pl.ds(m0, tt), W_GROUP:2 * W_GROUP])

    z8 = jnp.zeros((HALO, W_GROUP), F32)
    for scr in (sa_scr, sb_scr, sc_scr, sd_scr):
        scr[0:HALO, :] = z8
        scr[tt + 3 * HALO:tt + 4 * HALO, :] = z8
    c0 = 2 * W_GROUP
    sa_scr[HALO:HALO + ne, :] = (ext_scr[pl.ds(HALO - 1, ne), c0:c0 + W_GROUP]
                                 + ext_scr[pl.ds(HALO, ne), c0:c0 + W_GROUP])
    sb_scr[HALO:HALO + ne, :] = sa_scr[pl.ds(HALO - 1, ne), :] + sa_scr[pl.ds(HALO + 1, ne), :]
    sc_scr[HALO:HALO + ne, :] = sb_scr[pl.ds(HALO - 2, ne), :] + sb_scr[pl.ds(HALO + 2, ne), :]
    p16 = sc_scr[pl.ds(m0 - 4, tt), :] + sc_scr[pl.ds(m0 + 4, tt), :]
    p2 = sa_scr[pl.ds(m0, tt), :]
    p4 = sb_scr[pl.ds(m0, tt), :]
    p8 = sc_scr[pl.ds(m0, tt), :]
    grp = lax.broadcasted_iota(jnp.int32, (tt, W_GROUP), 1) // (W_GROUP // len(POOL_WINDOWS))
    tpos = t0 + lax.broadcasted_iota(jnp.int32, (tt, W_GROUP), 0)
    sums = jnp.where(grp == 0, p2, jnp.where(grp == 1, p4, jnp.where(grp == 2, p8, p16)))
    half = jnp.where(grp == 0, 1, jnp.where(grp == 1, 2, jnp.where(grp == 2, 4, 8)))
    cnt = jnp.minimum(tpos + half, n_seq) - jnp.maximum(tpos - half, 0)
    bx = ext_scr[pl.ds(m0, tt), c0:c0 + W_GROUP]
    dpool = sums / cnt.astype(F32) - bx
    y_b = jnp.dot(dpool.astype(BF16), poolw_ref[...], preferred_element_type=F32) * pscale_ref[...]
    ybcd_ref[:, 0:W_GROUP] = y_b.astype(BF16)

    c_cb, c_cc, c_ch = 3 * W_GROUP, 4 * W_GROUP, 5 * W_GROUP
    sd_scr[HALO:HALO + ne, :] = (ext_scr[pl.ds(HALO, ne), c_cc:c_cc + W_GROUP]
                                 * ext_scr[pl.ds(HALO, ne), c_ch:c_ch + W_GROUP])
    cwc = cwc_ref[...]
    conv = jnp.zeros((tt, W_GROUP), F32)
    for k in range(3):
        conv = conv + cwc[k:k + 1, :] * sd_scr[pl.ds(m0 - 1 + k, tt), :]
    y_c = ext_scr[pl.ds(m0, tt), c_cb:c_cb + W_GROUP] * conv
    ybcd_ref[:, W_GROUP:2 * W_GROUP] = y_c.astype(BF16)

    c_du, c_dv = 6 * W_GROUP, 7 * W_GROUP
    u = _gelu(ext_scr[pl.ds(m0, tt), c_du:c_du + W_GROUP])
    gv = _gelu(ext_scr[pl.ds(m0, tt), c_dv:c_dv + W_GROUP])
    xc = gv - jnp.mean(gv, axis=-1, keepdims=True)
    v = xc * lax.rsqrt(jnp.mean(xc * xc, axis=-1, keepdims=True) + EPS) * sgug_ref[...]
    head = lax.broadcasted_iota(jnp.int32, (SGU_CHUNK, W_GROUP), 1) // (W_GROUP // SGU_HEADS)
    for c in range(tt // SGU_CHUNK):
        vc = v[c * SGU_CHUNK:(c + 1) * SGU_CHUNK, :]
        stack = jnp.concatenate([jnp.where(head == h, vc, 0.0) for h in range(SGU_HEADS)], axis=0)
        mixed = jnp.dot(sguw_ref[...], stack.astype(BF16), preferred_element_type=F32) + sgub_ref[...]
        y_d = u[c * SGU_CHUNK:(c + 1) * SGU_CHUNK, :] * mixed
        ybcd_ref[c * SGU_CHUNK:(c + 1) * SGU_CHUNK, 2 * W_GROUP:3 * W_GROUP] = y_d.astype(BF16)


def _mix_in_call(z, pos_tab, mods, lw, layer, nct, tps, n_batch):
    n, d = z.shape
    tt = TOK_TILE
    nt = n // tt
    hb = tt // HALO
    n_pos = pos_tab.shape[0]

    def mrow(i):
        return jnp.where(i < nct, n_batch, (i - nct) // tps)

    def pblk(i):
        return jnp.where(i < nct, 0, 1 + (i - nct) % tps)

    full = lambda shape: pl.BlockSpec(shape, lambda i: (0,) * len(shape))
    in_specs = [
        pl.BlockSpec((HALO, d), lambda i: (jnp.maximum(i * hb - 1, 0), 0)),
        pl.BlockSpec((tt, d), lambda i: (i, 0)),
        pl.BlockSpec((HALO, d), lambda i: (jnp.minimum((i + 1) * hb, n // HALO - 1), 0)),
        pl.BlockSpec((HALO, d), lambda i: (jnp.maximum(pblk(i) * hb - 1, 0), 0)),
        pl.BlockSpec((tt, d), lambda i: (pblk(i), 0)),
        pl.BlockSpec((HALO, d), lambda i: (jnp.minimum((pblk(i) + 1) * hb, n_pos // HALO - 1), 0)),
        pl.BlockSpec((None, N_MOD, d), lambda i: (mrow(i), 0, 0)),
        full((1, d)),
        pl.BlockSpec((None, d, PROJ_DIM), lambda i: (layer, 0, 0)),
        full((4, W_GROUP)), full((1, W_GROUP)), full((3, W_GROUP)),
        full((W_GROUP, W_GROUP)), full((1, W_GROUP)), full((1, W_GROUP)),
        full((SGU_CHUNK, SGU_HEADS * SGU_CHUNK)), full((SGU_CHUNK, W_GROUP)),
    ]
    out_specs = [pl.BlockSpec((tt, W_GROUP), lambda i: (i, 0)),
                 pl.BlockSpec((tt, W_GROUP), lambda i: (i, 0)),
                 pl.BlockSpec((tt, 3 * W_GROUP), lambda i: (i, 0))]
    out_shape = [jax.ShapeDtypeStruct((n, W_GROUP), F32),
                 jax.ShapeDtypeStruct((n, W_GROUP), F32),
                 jax.ShapeDtypeStruct((n, 3 * W_GROUP), BF16)]
    er = tt + 4 * HALO
    return pl.pallas_call(
        functools.partial(_mix_in_kernel, nct, tps),
        grid=(nt,),
        in_specs=in_specs, out_specs=out_specs, out_shape=out_shape,
        scratch_shapes=[pltpu.VMEM((er, PROJ_DIM), F32)] + [pltpu.VMEM((er, W_GROUP), F32)] * 4,
        compiler_params=pltpu.CompilerParams(vmem_limit_bytes=VMEM_LIMIT),
        name="mix_in",
    )(z, z, z, pos_tab, pos_tab, pos_tab, mods, lw["g_mix"], lw["w_in"], lw["conv_a_w"],
      lw["conv_a_b"], lw["conv_c_w"], lw["pool_w"], lw["pool_scale"], lw["sgu_g"],
      lw["sgu_w"], lw["sgu_b"])


def _chunk_scan(a, b, reverse):
    t = a.shape[0]
    row = lax.broadcasted_iota(jnp.int32, a.shape, 0)
    d = 1
    while d < t:
        shift = t - d if reverse else d
        ok = (row < t - d) if reverse else (row >= d)
        a_s = jnp.where(ok, pltpu.roll(a, shift, axis=0), 1.0)
        b_s = jnp.where(ok, pltpu.roll(b, shift, axis=0), 0.0)
        b = b + a * b_s
        a = a * a_s
        d *= 2
    return a, b


def _scan_kernel(xf_ref, xb_ref, wg_ref, bg_ref, lam_ref, hf_ref, hb_ref, carry_ref):
    tt = TOK_TILE
    s = pl.program_id(1)

    @pl.when(s == 0)
    def _():
        carry_ref[...] = jnp.zeros_like(carry_ref)

    for d, (x_ref, o_ref) in enumerate(((xf_ref, hf_ref), (xb_ref, hb_ref))):
        xa = x_ref[...]
        g = jnp.dot(xa.astype(BF16), wg_ref[d], preferred_element_type=F32) + bg_ref[d]
        r = jax.nn.sigmoid(g[:, 0:W_GROUP])
        gi = jax.nn.sigmoid(g[:, W_GROUP:2 * W_GROUP])
        neg_lam = -lam_ref[d:d + 1, :]
        softplus = jnp.maximum(neg_lam, 0.0) + jnp.log1p(jnp.exp(-jnp.abs(neg_lam)))
        log_a = (-LRU_C) * r * softplus
        a = jnp.exp(log_a)
        b = jnp.sqrt(1.0 - a * a) * (gi * xa)
        a_cum, h0 = _chunk_scan(a, b, reverse=(d == 1))
        h = h0 + a_cum * carry_ref[d:d + 1, :]
        o_ref[...] = h
        edge = 0 if d == 1 else tt - 1
        carry_ref[d:d + 1, :] = h[edge:edge + 1, :]


def _scan_call(xa, lw, nct, tps, n_batch):
    n, c = xa.shape
    tt = TOK_TILE
    fwd = lambda b, s: (jnp.where(s == 0, b, nct + b * tps + s - 1), 0)
    bwd = lambda b, s: (jnp.where(s == 0, b, nct + b * tps + tps - s), 0)
    full = lambda shape: pl.BlockSpec(shape, lambda b, s: (0,) * len(shape))
    return pl.pallas_call(
        _scan_kernel,
        grid=(n_batch, tps + 1),
        in_specs=[pl.BlockSpec((tt, c), fwd), pl.BlockSpec((tt, c), bwd),
                  full((2, c, 2 * c)), full((2, 1, 2 * c)), full((2, c))],
        out_specs=[pl.BlockSpec((tt, c), fwd), pl.BlockSpec((tt, c), bwd)],
        out_shape=[jax.ShapeDtypeStruct((n, c), F32)] * 2,
        scratch_shapes=[pltpu.VMEM((2, c), F32)],
        compiler_params=pltpu.CompilerParams(vmem_limit_bytes=VMEM_LIMIT),
        name="lru_scan",
    )(xa, xa, lw["lru_wg"], lw["lru_bg"], lw["lru_l"])


def _sort16_network():
    def merge(lo, hi, r):
        step = r * 2
        if step < hi - lo:
            yield from merge(lo, hi, step)
            yield from merge(lo + r, hi, step)
            yield from [(i, i + r) for i in range(lo + r, hi - r, step)]
        else:
            yield (lo, lo + r)

    def sort(lo, hi):
        if hi - lo >= 1:
            mid = lo + (hi - lo) // 2
            yield from sort(lo, mid)
            yield from sort(mid + 1, hi)
            yield from merge(lo, hi, 1)

    return tuple(sort(0, PEER_TOPK - 1))


_SORT16 = _sort16_network()


def _top16_of_columns(cols):
    cols = list(cols)
    rows = []
    for r in range(PEER_TOPK):
        mx = jnp.max(cols[0], axis=0, keepdims=True)
        rows.append(mx)
        if r + 1 < PEER_TOPK:
            hit = cols[0] == mx
            depth = PEER_TOPK - r
            for d in range(depth - 1):
                cols[d] = jnp.where(hit, cols[d + 1], cols[d])
            cols[depth - 1] = jnp.where(hit, NEG_BIG, cols[depth - 1])
    return rows


def _top16_of_keys(s):
    cols = [s[8 * g:8 * g + 8, :] for g in range(N_KEYS // 8)]
    for a, b in _SORT16:
        cols[a], cols[b] = jnp.maximum(cols[a], cols[b]), jnp.minimum(cols[a], cols[b])
    return _top16_of_columns(cols)


def _rank16(s, v):
    m8 = s >= v[7]
    m4 = s >= jnp.where(m8, v[3], v[11])
    m2 = s >= jnp.where(m8, jnp.where(m4, v[1], v[5]), jnp.where(m4, v[9], v[13]))
    m1 = s >= jnp.where(m8, jnp.where(m4, jnp.where(m2, v[0], v[2]), jnp.where(m2, v[4], v[6])),
                        jnp.where(m4, jnp.where(m2, v[8], v[10]), jnp.where(m2, v[12], v[14])))
    rank = (jnp.where(m8, 0.0, 8.0) + jnp.where(m4, 0.0, 4.0)
            + jnp.where(m2, 0.0, 2.0) + jnp.where(m1, 0.0, 1.0))
    return jnp.where(s >= v[15], rank, float(PEER_TOPK))


def _mid_kernel(z_ref, hf_ref, hb_ref, gag_ref, ybcd_ref, mod_ref, wout_ref, gffn_ref, wqt_ref,
                k1_ref, k2_ref, zo_ref, ft_ref, c1_ref, e1_ref, r2_ref, e2_ref, qt_scr):
    ya = ((hf_ref[...] + hb_ref[...]) * gag_ref[...]).astype(BF16)
    y = jnp.concatenate([ya, ybcd_ref[...]], axis=1)
    o = jnp.dot(y, wout_ref[...], preferred_element_type=F32)
    z = z_ref[...] + mod_ref[2:3, :] * o
    zo_ref[...] = z
    f = _rms(z, gffn_ref[...]) * (1.0 + mod_ref[4:5, :]) + mod_ref[3:4, :]
    ft = f.T.astype(BF16)
    ft_ref[...] = ft
    qt_scr[...] = jnp.dot(wqt_ref[...], ft, preferred_element_type=F32)
    lanes = 128

    def head_body(h, carry):
        base = pl.multiple_of(h * (2 * PEER_HALF), 2 * PEER_HALF)
        q1 = qt_scr[pl.ds(base, PEER_HALF), :].astype(BF16)
        q2 = qt_scr[pl.ds(base + PEER_HALF, PEER_HALF), :].astype(BF16)
        s1_all = jnp.dot(k1_ref[...], q1, preferred_element_type=F32)
        s2_all = jnp.dot(k2_ref[...], q2, preferred_element_type=F32)
        for lt in range(MID_TILE // lanes):
            sl = slice(lt * lanes, (lt + 1) * lanes)
            s1 = s1_all[:, sl]
            s2 = s2_all[:, sl]
            v1r = _top16_of_keys(s1)
            v2r = _top16_of_keys(s2)
            r2 = _rank16(s2, v2r)
            v1 = jnp.concatenate(v1r, axis=0)
            v2 = jnp.concatenate(v2r, axis=0)
            tops = _top16_of_columns([v1 + v2r[b] for b in range(PEER_TOPK)])
            m = tops[0]
            tau = tops[PEER_TOPK - 1]
            zsum = jnp.zeros_like(m)
            for row in tops:
                zsum = zsum + jnp.exp(row - m)
            theta = jnp.full(v2.shape, -NEG_BIG, F32)
            for a in range(PEER_TOPK):
                theta = jnp.minimum(theta, jnp.where(v1r[a] + v2 >= tau, v1r[a], -NEG_BIG))
            th = [theta[b:b + 1, :] for b in range(8)]
            m4 = s1 >= th[3]
            m2 = s1 >= jnp.where(m4, th[5], th[1])
            m1 = s1 >= jnp.where(m4, jnp.where(m2, th[6], th[4]), jnp.where(m2, th[2], th[0]))
            c1 = jnp.where(m4, 4.0, 0.0) + jnp.where(m2, 2.0, 0.0) + jnp.where(m1, 1.0, 0.0)
            c1 = jnp.where(s1 >= th[7], 8.0, c1)
            n_hi = jnp.sum(jnp.where(theta[8:16, :] < 1.0e38, 1.0, 0.0), axis=0, keepdims=True)
            c1 = c1 + jnp.where(s1 >= v1r[0], n_hi, 0.0)
            c1_ref[h, :, sl] = c1
            e1_ref[h, :, sl] = jnp.exp(s1 - v1r[0]) * (1.0 / zsum)
            r2_ref[h, :, sl] = r2.astype(BF16)
            e2_ref[h, :, sl] = jnp.exp(s2 - v2r[0]).astype(BF16)
        return carry

    def head_group(hg, carry):
        for j in range(MID_HEADS_PER_ITER):
            head_body(MID_HEADS_PER_ITER * hg + j, carry)
        return carry

    lax.fori_loop(0, PEER_HEADS // MID_HEADS_PER_ITER, head_group, 0)


def _mid_call(z, hf, hb, gag, ybcd, mods, lw, layer, n_ctx_tok, seq, n_batch, skip_ctx):
    n, d = z.shape
    tt = MID_TILE
    nct = n_ctx_tok // tt
    tps = seq // tt
    tile_off = nct if skip_ctx else 0
    nt = n // tt - tile_off
    nq = PEER_HEADS * 2 * PEER_HALF

    def mrow(i):
        j = i + tile_off
        return jnp.where(j < nct, n_batch, (j - nct) // tps)

    full = lambda shape: pl.BlockSpec(shape, lambda i: (0,) * len(shape))
    tok = lambda w: pl.BlockSpec((tt, w), lambda i: (i + tile_off, 0))
    in_specs = [tok(d), tok(W_GROUP), tok(W_GROUP), tok(W_GROUP), tok(3 * W_GROUP),
                pl.BlockSpec((None, N_MOD, d), lambda i: (mrow(i), 0, 0)),
                pl.BlockSpec((None, d, d), lambda i: (layer, 0, 0)), full((1, d)),
                pl.BlockSpec((None, nq, d), lambda i: (layer, 0, 0)),
                full((N_KEYS, PEER_HALF)), full((N_KEYS, PEER_HALF))]
    hk = lambda: pl.BlockSpec((PEER_HEADS, N_KEYS, tt), lambda i: (0, 0, i + tile_off))
    out_specs = [tok(d), pl.BlockSpec((d, tt), lambda i: (0, i + tile_off)), hk(), hk(), hk(), hk()]
    hk_shape = lambda dt: jax.ShapeDtypeStruct((PEER_HEADS, N_KEYS, n), dt)
    out_shape = [jax.ShapeDtypeStruct((n, d), F32), jax.ShapeDtypeStruct((d, n), BF16),
                 hk_shape(F32), hk_shape(F32), hk_shape(BF16), hk_shape(BF16)]
    return pl.pallas_call(
        _mid_kernel,
        grid=(nt,),
        in_specs=in_specs, out_specs=out_specs, out_shape=out_shape,
        scratch_shapes=[pltpu.VMEM((nq, tt), F32)],
        compiler_params=pltpu.CompilerParams(vmem_limit_bytes=VMEM_LIMIT),
        name="mid",
    )(z, hf, hb, gag, ybcd, mods, lw["w_out"], lw["g_ffn"], lw["wq_t"], lw["k1"], lw["k2"])


def _peer_kernel(final, ft_ref, c1_ref, e1_ref, r2_ref, e2_ref, u_ref, vt_ref, z_ref, mod_ref,
                 gfin_ref, zo_ref, acc_ref, a_scr, h_scr0, h_scr1, w_scr0, w_scr1):
    c = pl.program_id(1)
    tl = PEER_LANES
    sub = PEER_SUB
    pk = BF16_ROWS
    n_sub = PEER_ECHUNK // sub
    h_scr = (h_scr0, h_scr1)
    w_scr = (w_scr0, w_scr1)

    @pl.when(c == 0)
    def _():
        acc_ref[...] = jnp.zeros_like(acc_ref)

    def pre_act(k):
        h_scr[k % 2][...] = jnp.dot(u_ref[k * sub:(k + 1) * sub, :], ft_ref[...],
                                    preferred_element_type=F32)

    def gates(k):
        for il in range(sub // N_KEYS):
            i_loc = k * (sub // N_KEYS) + il
            for lt in range(PEER_TILE // tl):
                ls = slice(lt * tl, (lt + 1) * tl)
                w = jnp.zeros((N_KEYS // pk, pk, tl), BF16)
                for h in range(PEER_HEADS):
                    c1 = jnp.broadcast_to(c1_ref[h, i_loc:i_loc + 1, ls], (pk, tl)).astype(BF16)
                    e1 = jnp.broadcast_to(e1_ref[h, i_loc:i_loc + 1, ls], (pk, tl)).astype(BF16)
                    sel = jnp.where(r2_ref[h, :, :, ls] < c1[None], e2_ref[h, :, :, ls], 0.0)
                    w = w + sel * e1[None]
                w_scr[k % 2][il * N_KEYS:(il + 1) * N_KEYS, ls] = w.reshape(N_KEYS, tl)

    def activate(k):
        g = _gelu_sigmoid(h_scr[k % 2][...].astype(BF16))
        a_scr[k * sub:(k + 1) * sub, :] = g * w_scr[k % 2][...]

    def project(k):
        acc_ref[...] += jnp.dot(vt_ref[:, k * sub:(k + 1) * sub], a_scr[k * sub:(k + 1) * sub, :],
                                preferred_element_type=F32)

    pre_act(0)
    gates(0)
    for k in range(n_sub):
        if k + 1 < n_sub:
            pre_act(k + 1)
        if k >= 1:
            project(k - 1)
        activate(k)
        if k + 1 < n_sub:
            gates(k + 1)
    project(n_sub - 1)

    @pl.when(c == pl.num_programs(1) - 1)
    def _():
        z = z_ref[...] + mod_ref[5:6, :] * acc_ref[...].T
        zo_ref[...] = _rms(z, gfin_ref[...]) if final else z


def _peer_call(z, ft, c1, e1, r2, e2, mods, lw, layer, g_final, n_ctx_tok, seq, n_batch, final):
    n, d = z.shape
    t = PEER_TILE
    ec = PEER_ECHUNK
    n_exp = lw["u"].shape[1]
    nct = n_ctx_tok // t
    tps = seq // t
    ic = ec // N_KEYS
    pk = BF16_ROWS
    off = nct if final else 0
    r2 = r2.reshape(PEER_HEADS, N_KEYS // pk, pk, n)
    e2 = e2.reshape(PEER_HEADS, N_KEYS // pk, pk, n)

    def mrow(i):
        j = i + off
        return jnp.where(j < nct, n_batch, (j - nct) // tps)

    in_specs = [
        pl.BlockSpec((d, t), lambda i, c: (0, i + off)),
        pl.BlockSpec((PEER_HEADS, ic, t), lambda i, c: (0, c, i + off)),
        pl.BlockSpec((PEER_HEADS, ic, t), lambda i, c: (0, c, i + off)),
        pl.BlockSpec((PEER_HEADS, N_KEYS // pk, pk, t), lambda i, c: (0, 0, 0, i + off)),
        pl.BlockSpec((PEER_HEADS, N_KEYS // pk, pk, t), lambda i, c: (0, 0, 0, i + off)),
        pl.BlockSpec((None, ec, d), lambda i, c: (layer, c, 0)),
        pl.BlockSpec((None, d, ec), lambda i, c: (layer, 0, c)),
        pl.BlockSpec((t, d), lambda i, c: (i + off, 0)),
        pl.BlockSpec((None, N_MOD, d), lambda i, c: (mrow(i), 0, 0)),
        pl.BlockSpec((1, d), lambda i, c: (0, 0)),
    ]
    return pl.pallas_call(
        functools.partial(_peer_kernel, final),
        grid=(n // t - off, n_exp // ec),
        in_specs=in_specs,
        out_specs=pl.BlockSpec((t, d), lambda i, c: (i, 0)),
        out_shape=jax.ShapeDtypeStruct((n - off * t, d), F32),
        scratch_shapes=[pltpu.VMEM((d, t), F32), pltpu.VMEM((ec, t), BF16)]
        + [pltpu.VMEM((PEER_SUB, t), F32)] * 2 + [pltpu.VMEM((PEER_SUB, t), BF16)] * 2,
        compiler_params=pltpu.CompilerParams(vmem_limit_bytes=VMEM_LIMIT),
        name="peer",
    )(ft, c1, e1, r2, e2, lw["u"], lw["v_t"], z, mods, g_final)


def _block_diag(w):
    h, a, b = w.shape
    eye = jnp.eye(h, dtype=w.dtype)
    return (eye[:, None, :, None] * w[:, :, None, :]).reshape(h * a, h * b)


def _grid_sincos(n, dim):
    rows = n // GRID_W
    quarter = dim // 4
    omega = POS_BASE ** (-jnp.arange(quarter, dtype=F32) / quarter)

    def axis_emb(p):
        ang = p[:, None] * omega[None, :]
        return jnp.concatenate([jnp.sin(ang), jnp.cos(ang)], axis=-1)

    row_emb = jnp.repeat(axis_emb(jnp.arange(rows, dtype=F32)), GRID_W, axis=0)
    col_emb = jnp.tile(axis_emb(jnp.arange(GRID_W, dtype=F32)), (rows, 1))
    return jnp.concatenate([row_emb, col_emb], axis=-1)


def kernel(x, c, ctx, c_ctx, w_ada, b_ada, g_mix, g_ffn, w_in, w_out, conv_a_w, conv_a_b, lru_l, lru_wr, lru_br, lru_wi, lru_bi, pool_w, pool_scale, conv_c_w, sgu_g, sgu_w, sgu_b, peer_wq, peer_k1, peer_k2, peer_u, peer_v, g_final):
    bsz, seq, d = x.shape
    ctx_len = ctx.shape[1]
    depth = w_ada.shape[0]
    assert ctx_len == TOK_TILE and seq % TOK_TILE == 0 and seq % PEER_TILE == 0 and seq % MID_TILE == 0
    assert (bsz * ctx_len) % PEER_TILE == 0 and (bsz * ctx_len) % MID_TILE == 0 and bsz + 1 <= 8
    n_ctx_tok = bsz * ctx_len
    nct = n_ctx_tok // TOK_TILE
    tps = seq // TOK_TILE

    cvec = jnp.concatenate([c, c_ctx[None, :], jnp.zeros((8 - bsz - 1, d), F32)], axis=0)
    mods_all = _ada_call(cvec, w_ada, b_ada).reshape(depth, 8, N_MOD, d)

    pos_tab = jnp.concatenate([jnp.zeros((TOK_TILE, d), F32), _grid_sincos(seq, d)], axis=0)
    z = jnp.concatenate([ctx.reshape(n_ctx_tok, d), x.reshape(bsz * seq, d)], axis=0)

    stacked = {
        "w_in": w_in.astype(BF16), "w_out": w_out.astype(BF16),
        "wq_t": jnp.swapaxes(peer_wq, 1, 2).astype(BF16),
        "u": peer_u.astype(BF16), "v_t": jnp.swapaxes(peer_v, 1, 2).astype(BF16),
    }

    for l in range(depth):
        wg = jnp.stack([jnp.concatenate([_block_diag(lru_wr[l, dd]), _block_diag(lru_wi[l, dd])], axis=1)
                        for dd in range(2)]).astype(BF16)
        bg = jnp.stack([jnp.concatenate([lru_br[l, dd].reshape(1, -1), lru_bi[l, dd].reshape(1, -1)], axis=1)
                        for dd in range(2)])
        lw = {
            "g_mix": g_mix[l].reshape(1, d), "g_ffn": g_ffn[l].reshape(1, d),
            "conv_a_w": conv_a_w[l], "conv_a_b": conv_a_b[l].reshape(1, -1),
            "conv_c_w": conv_c_w[l],
            "pool_w": _block_diag(pool_w[l]).astype(BF16), "pool_scale": pool_scale[l].reshape(1, -1),
            "sgu_g": sgu_g[l].reshape(1, -1),
            "sgu_w": jnp.transpose(sgu_w[l], (1, 0, 2)).reshape(SGU_CHUNK, SGU_HEADS * SGU_CHUNK).astype(BF16),
            "sgu_b": jnp.repeat(sgu_b[l].T, W_GROUP // SGU_HEADS, axis=1),
            "lru_wg": wg, "lru_bg": bg, "lru_l": lru_l[l],
            "k1": peer_k1[l].astype(BF16), "k2": peer_k2[l].astype(BF16),
        }
        lw.update(stacked)
        mods = mods_all[l]
        xa, gag, ybcd = _mix_in_call(z, pos_tab, mods, lw, l, nct, tps, bsz)
        hf, hb = _scan_call(xa, lw, nct, tps, bsz)
        last = l == depth - 1
        z, ft, c1, e1, r2, e2 = _mid_call(z, hf, hb, gag, ybcd, mods, lw, l, n_ctx_tok, seq, bsz, last)
        z = _peer_call(z, ft, c1, e1, r2, e2, mods, lw, l, g_final.reshape(1, d), n_ctx_tok, seq,
                       bsz, last)

    return z.reshape(bsz, seq, d)
```

```python
import functools

import jax
import jax.numpy as jnp
from jax import lax
from jax.experimental import pallas as pl
from jax.experimental.pallas import tpu as pltpu

F32 = jnp.float32
BF16 = jnp.bfloat16

W_GROUP = 256
LRU_HEADS = 4
LRU_C = 8.0
POOL_WINDOWS = (2, 4, 8, 16)
SGU_CHUNK = 128
SGU_HEADS = 4
PROJ_DIM = 8 * W_GROUP
N_KEYS = 128
PEER_HEADS = 8
PEER_HALF = 128
PEER_TOPK = 16
N_MOD = 6
EPS = 1e-6
POS_BASE = 10000.0
GRID_W = 64

TOK_TILE = 256
MID_TILE = 512
MID_HEADS_PER_ITER = 4
HALO = 8
PEER_TILE = 512
PEER_ECHUNK = 2048
PEER_SUB = 512
PEER_LANES = 256
BF16_ROWS = 16
NEG_BIG = -3.0e38
VMEM_LIMIT = 56 * 1024 * 1024


def _gelu(x):
    return 0.5 * x * (1.0 + jnp.tanh(0.7978845608028654 * (x + 0.044715 * x * x * x)))


def _gelu_sigmoid(x):
    k1 = -2.0 * 0.7978845608028654 * 1.4426950408889634
    return x / (1.0 + jnp.exp2(x * (x * x * (k1 * 0.044715) + k1)))


def _rms(x, g):
    return x * lax.rsqrt(jnp.mean(x * x, axis=-1, keepdims=True) + EPS) * g


def _ada_kernel(c_ref, w_ref, b_ref, o_ref):
    cv = c_ref[...]
    s = cv * jax.nn.sigmoid(cv)
    o_ref[...] = jnp.dot(s.astype(BF16), w_ref[...].astype(BF16),
                         preferred_element_type=F32) + b_ref[...]


def _ada_call(cvec, w_ada, b_ada):
    depth, d, nm = w_ada.shape
    tn = 1536
    return pl.pallas_call(
        _ada_kernel,
        grid=(depth, nm // tn),
        in_specs=[pl.BlockSpec((8, d), lambda l, j: (0, 0)),
                  pl.BlockSpec((None, d, tn), lambda l, j: (l, 0, j)),
                  pl.BlockSpec((None, 1, tn), lambda l, j: (l, 0, j))],
        out_specs=pl.BlockSpec((None, 8, tn), lambda l, j: (l, 0, j)),
        out_shape=jax.ShapeDtypeStruct((depth, 8, nm), F32),
        compiler_params=pltpu.CompilerParams(vmem_limit_bytes=VMEM_LIMIT),
        name="ada",
    )(cvec, w_ada, b_ada.reshape(depth, 1, nm))


def _mix_in_kernel(nct, tps, zp_ref, z_ref, zn_ref, pp_ref, p_ref, pn_ref, mod_ref, gmix_ref,
                   win_ref, cwa_ref, cba_ref, cwc_ref, poolw_ref, pscale_ref, sgug_ref,
                   sguw_ref, sgub_ref, xa_ref, gag_ref, ybcd_ref,
                   ext_scr, sa_scr, sb_scr, sc_scr, sd_scr):
    tt = TOK_TILE
    i = pl.program_id(0)
    is_ctx = i < nct
    pos_tile = jnp.where(is_ctx, 0, (i - nct) % tps)
    seq_tiles = jnp.where(is_ctx, 1, tps)
    first = pos_tile == 0
    last = pos_tile == seq_tiles - 1
    t0 = pos_tile * tt
    n_seq = seq_tiles * tt

    z_ext = jnp.concatenate([zp_ref[...], z_ref[...], zn_ref[...]], axis=0)
    pos_ext = jnp.concatenate([pp_ref[...], p_ref[...], pn_ref[...]], axis=0)
    a = _rms(z_ext, gmix_ref[...])
    a = a * (1.0 + mod_ref[1:2, :]) + mod_ref[0:1, :] + pos_ext
    row = lax.broadcasted_iota(jnp.int32, (tt + 2 * HALO, 1), 0)
    keep_prev = jnp.where(first, 0.0, 1.0)
    keep_next = jnp.where(last, 0.0, 1.0)
    keep = jnp.where(row < HALO, keep_prev, jnp.where(row >= tt + HALO, keep_next, 1.0))
    a = a * keep
    p = jnp.dot(a.astype(BF16), win_ref[...], preferred_element_type=F32)

    zeros8 = jnp.zeros((HALO, PROJ_DIM), F32)
    ext_scr[0:HALO, :] = zeros8
    ext_scr[tt + 3 * HALO:tt + 4 * HALO, :] = zeros8
    ext_scr[HALO:tt + 3 * HALO, :] = p
    m0 = 2 * HALO
    ne = tt + 2 * HALO

    cwa = cwa_ref[...]
    xa = cba_ref[...]
    for k in range(4):
        xa = xa + cwa[k:k + 1, :] * ext_scr[pl.ds(m0 - 2 + k, tt), 0:W_GROUP]
    xa_ref[...] = xa
    gag_ref[...] = _gelu(ext_scr[pl.ds(m0, tt), W_GROUP:2 * W_GROUP])

    z8 = jnp.zeros((HALO, W_GROUP), F32)
    for scr in (sa_scr, sb_scr, sc_scr, sd_scr):
        scr[0:HALO, :] = z8
        scr[tt + 3 * HALO:tt + 4 * HALO, :] = z8
    c0 = 2 * W_GROUP
    sa_scr[HALO:HALO + ne, :] = (ext_scr[pl.ds(HALO - 1, ne), c0:c0 + W_GROUP]
                                 + ext_scr[pl.ds(HALO, ne), c0:c0 + W_GROUP])
    sb_scr[HALO:HALO + ne, :] = sa_scr[pl.ds(HALO - 1, ne), :] + sa_scr[pl.ds(HALO + 1, ne), :]
    sc_scr[HALO:HALO + ne, :] = sb_scr[pl.ds(HALO - 2, ne), :] + sb_scr[pl.ds(HALO + 2, ne), :]
    p16 = sc_scr[pl.ds(m0 - 4, tt), :] + sc_scr[pl.ds(m0 + 4, tt), :]
    p2 = sa_scr[pl.ds(m0, tt), :]
    p4 = sb_scr[pl.ds(m0, tt), :]
    p8 = sc_scr[pl.ds(m0, tt), :]
    grp = lax.broadcasted_iota(jnp.int32, (tt, W_GROUP), 1) // (W_GROUP // len(POOL_WINDOWS))
    tpos = t0 + lax.broadcasted_iota(jnp.int32, (tt, W_GROUP), 0)
    sums = jnp.where(grp == 0, p2, jnp.where(grp == 1, p4, jnp.where(grp == 2, p8, p16)))
    half = jnp.where(grp == 0, 1, jnp.where(grp == 1, 2, jnp.where(grp == 2, 4, 8)))
    cnt = jnp.minimum(tpos + half, n_seq) - jnp.maximum(tpos - half, 0)
    bx = ext_scr[pl.ds(m0, tt), c0:c0 + W_GROUP]
    dpool = sums / cnt.astype(F32) - bx
    y_b = jnp.dot(dpool.astype(BF16), poolw_ref[...], preferred_element_type=F32) * pscale_ref[...]
    ybcd_ref[:, 0:W_GROUP] = y_b.astype(BF16)

    c_cb, c_cc, c_ch = 3 * W_GROUP, 4 * W_GROUP, 5 * W_GROUP
    sd_scr[HALO:HALO + ne, :] = (ext_scr[pl.ds(HALO, ne), c_cc:c_cc + W_GROUP]
                                 * ext_scr[pl.ds(HALO, ne), c_ch:c_ch + W_GROUP])
    cwc = cwc_ref[...]
    conv = jnp.zeros((tt, W_GROUP), F32)
    for k in range(3):
        conv = conv + cwc[k:k + 1, :] * sd_scr[pl.ds(m0 - 1 + k, tt), :]
    y_c = ext_scr[pl.ds(m0, tt), c_cb:c_cb + W_GROUP] * conv
    ybcd_ref[:, W_GROUP:2 * W_GROUP] = y_c.astype(BF16)

    c_du, c_dv = 6 * W_GROUP, 7 * W_GROUP
    u = _gelu(ext_scr[pl.ds(m0, tt), c_du:c_du + W_GROUP])
    gv = _gelu(ext_scr[pl.ds(m0, tt), c_dv:c_dv + W_GROUP])
    xc = gv - jnp.mean(gv, axis=-1, keepdims=True)
    v = xc * lax.rsqrt(jnp.mean(xc * xc, axis=-1, keepdims=True) + EPS) * sgug_ref[...]
    head = lax.broadcasted_iota(jnp.int32, (SGU_CHUNK, W_GROUP), 1) // (W_GROUP // SGU_HEADS)
    for c in range(tt // SGU_CHUNK):
        vc = v[c * SGU_CHUNK:(c + 1) * SGU_CHUNK, :]
        stack = jnp.concatenate([jnp.where(head == h, vc, 0.0) for h in range(SGU_HEADS)], axis=0)
        mixed = jnp.dot(sguw_ref[...], stack.astype(BF16), preferred_element_type=F32) + sgub_ref[...]
        y_d = u[c * SGU_CHUNK:(c + 1) * SGU_CHUNK, :] * mixed
        ybcd_ref[c * SGU_CHUNK:(c + 1) * SGU_CHUNK, 2 * W_GROUP:3 * W_GROUP] = y_d.astype(BF16)


def _mix_in_call(z, pos_tab, mods, lw, layer, nct, tps, n_batch):
    n, d = z.shape
    tt = TOK_TILE
    nt = n // tt
    hb = tt // HALO
    n_pos = pos_tab.shape[0]

    def mrow(i):
        return jnp.where(i < nct, n_batch, (i - nct) // tps)

    def pblk(i):
        return jnp.where(i < nct, 0, 1 + (i - nct) % tps)

    full = lambda shape: pl.BlockSpec(shape, lambda i: (0,) * len(shape))
    in_specs = [
        pl.BlockSpec((HALO, d), lambda i: (jnp.maximum(i * hb - 1, 0), 0)),
        pl.BlockSpec((tt, d), lambda i: (i, 0)),
        pl.BlockSpec((HALO, d), lambda i: (jnp.minimum((i + 1) * hb, n // HALO - 1), 0)),
        pl.BlockSpec((HALO, d), lambda i: (jnp.maximum(pblk(i) * hb - 1, 0), 0)),
        pl.BlockSpec((tt, d), lambda i: (pblk(i), 0)),
        pl.BlockSpec((HALO, d), lambda i: (jnp.minimum((pblk(i) + 1) * hb, n_pos // HALO - 1), 0)),
        pl.BlockSpec((None, N_MOD, d), lambda i: (mrow(i), 0, 0)),
        full((1, d)),
        pl.BlockSpec((None, d, PROJ_DIM), lambda i: (layer, 0, 0)),
        full((4, W_GROUP)), full((1, W_GROUP)), full((3, W_GROUP)),
        full((W_GROUP, W_GROUP)), full((1, W_GROUP)), full((1, W_GROUP)),
        full((SGU_CHUNK, SGU_HEADS * SGU_CHUNK)), full((SGU_CHUNK, W_GROUP)),
    ]
    out_specs = [pl.BlockSpec((tt, W_GROUP), lambda i: (i, 0)),
                 pl.BlockSpec((tt, W_GROUP), lambda i: (i, 0)),
                 pl.BlockSpec((tt, 3 * W_GROUP), lambda i: (i, 0))]
    out_shape = [jax.ShapeDtypeStruct((n, W_GROUP), F32),
                 jax.ShapeDtypeStruct((n, W_GROUP), F32),
                 jax.ShapeDtypeStruct((n, 3 * W_GROUP), BF16)]
    er = tt + 4 * HALO
    return pl.pallas_call(
        functools.partial(_mix_in_kernel, nct, tps),
        grid=(nt,),
        in_specs=in_specs, out_specs=out_specs, out_shape=out_shape,
        scratch_shapes=[pltpu.VMEM((er, PROJ_DIM), F32)] + [pltpu.VMEM((er, W_GROUP), F32)] * 4,
        compiler_params=pltpu.CompilerParams(vmem_limit_bytes=VMEM_LIMIT),
        name="mix_in",
    )(z, z, z, pos_tab, pos_tab, pos_tab, mods, lw["g_mix"], lw["w_in"], lw["conv_a_w"],
      lw["conv_a_b"], lw["conv_c_w"], lw["pool_w"], lw["pool_scale"], lw["sgu_g"],
      lw["sgu_w"], lw["sgu_b"])


def _chunk_scan(a, b, reverse):
    t = a.shape[0]
    row = lax.broadcasted_iota(jnp.int32, a.shape, 0)
    d = 1
    while d < t:
        shift = t - d if reverse else d
        ok = (row < t - d) if reverse else (row >= d)
        a_s = jnp.where(ok, pltpu.roll(a, shift, axis=0), 1.0)
        b_s = jnp.where(ok, pltpu.roll(b, shift, axis=0), 0.0)
        b = b + a * b_s
        a = a * a_s
        d *= 2
    return a, b


def _scan_kernel(xf_ref, xb_ref, wg_ref, bg_ref, lam_ref, hf_ref, hb_ref, carry_ref):
    tt = TOK_TILE
    s = pl.program_id(1)

    @pl.when(s == 0)
    def _():
        carry_ref[...] = jnp.zeros_like(carry_ref)

    for d, (x_ref, o_ref) in enumerate(((xf_ref, hf_ref), (xb_ref, hb_ref))):
        xa = x_ref[...]
        g = jnp.dot(xa.astype(BF16), wg_ref[d], preferred_element_type=F32) + bg_ref[d]
        r = jax.nn.sigmoid(g[:, 0:W_GROUP])
        gi = jax.nn.sigmoid(g[:, W_GROUP:2 * W_GROUP])
        neg_lam = -lam_ref[d:d + 1, :]
        softplus = jnp.maximum(neg_lam, 0.0) + jnp.log1p(jnp.exp(-jnp.abs(neg_lam)))
        log_a = (-LRU_C) * r * softplus
        a = jnp.exp(log_a)
        b = jnp.sqrt(1.0 - a * a) * (gi * xa)
        a_cum, h0 = _chunk_scan(a, b, reverse=(d == 1))
        h = h0 + a_cum * carry_ref[d:d + 1, :]
        o_ref[...] = h
        edge = 0 if d == 1 else tt - 1
        carry_ref[d:d + 1, :] = h[edge:edge + 1, :]


def _scan_call(xa, lw, nct, tps, n_batch):
    n, c = xa.shape
    tt = TOK_TILE
    fwd = lambda b, s: (jnp.where(s == 0, b, nct + b * tps + s - 1), 0)
    bwd = lambda b, s: (jnp.where(s == 0, b, nct + b * tps + tps - s), 0)
    full = lambda shape: pl.BlockSpec(shape, lambda b, s: (0,) * len(shape))
    return pl.pallas_call(
        _scan_kernel,
        grid=(n_batch, tps + 1),
        in_specs=[pl.BlockSpec((tt, c), fwd), pl.BlockSpec((tt, c), bwd),
                  full((2, c, 2 * c)), full((2, 1, 2 * c)), full((2, c))],
        out_specs=[pl.BlockSpec((tt, c), fwd), pl.BlockSpec((tt, c), bwd)],
        out_shape=[jax.ShapeDtypeStruct((n, c), F32)] * 2,
        scratch_shapes=[pltpu.VMEM((2, c), F32)],
        compiler_params=pltpu.CompilerParams(vmem_limit_bytes=VMEM_LIMIT),
        name="lru_scan",
    )(xa, xa, lw["lru_wg"], lw["lru_bg"], lw["lru_l"])


def _sort16_network():
    def merge(lo, hi, r):
        step = r * 2
        if step < hi - lo:
            yield from merge(lo, hi, step)
            yield from merge(lo + r, hi, step)
            yield from [(i, i + r) for i in range(lo + r, hi - r, step)]
        else:
            yield (lo, lo + r)

    def sort(lo, hi):
        if hi - lo >= 1:
            mid = lo + (hi - lo) // 2
            yield from sort(lo, mid)
            yield from sort(mid + 1, hi)
            yield from merge(lo, hi, 1)

    return tuple(sort(0, PEER_TOPK - 1))


_SORT16 = _sort16_network()


def _top16_of_columns(cols):
    cols = list(cols)
    rows = []
    for r in range(PEER_TOPK):
        mx = jnp.max(cols[0], axis=0, keepdims=True)
        rows.append(mx)
        if r + 1 < PEER_TOPK:
            hit = cols[0] == mx
            depth = PEER_TOPK - r
            for d in range(depth - 1):
                cols[d] = jnp.where(hit, cols[d + 1], cols[d])
            cols[depth - 1] = jnp.where(hit, NEG_BIG, cols[depth - 1])
    return rows


def _top16_of_keys(s):
    cols = [s[8 * g:8 * g + 8, :] for g in range(N_KEYS // 8)]
    for a, b in _SORT16:
        cols[a], cols[b] = jnp.maximum(cols[a], cols[b]), jnp.minimum(cols[a], cols[b])
    return _top16_of_columns(cols)


def _rank16(s, v):
    m8 = s >= v[7]
    m4 = s >= jnp.where(m8, v[3], v[11])
    m2 = s >= jnp.where(m8, jnp.where(m4, v[1], v[5]), jnp.where(m4, v[9], v[13]))
    m1 = s >= jnp.where(m8, jnp.where(m4, jnp.where(m2, v[0], v[2]), jnp.where(m2, v[4], v[6])),
                        jnp.where(m4, jnp.where(m2, v[8], v[10]), jnp.where(m2, v[12], v[14])))
    rank = (jnp.where(m8, 0.0, 8.0) + jnp.where(m4, 0.0, 4.0)
            + jnp.where(m2, 0.0, 2.0) + jnp.where(m1, 0.0, 1.0))
    return jnp.where(s >= v[15], rank, float(PEER_TOPK))


def _mid_kernel(z_ref, hf_ref, hb_ref, gag_ref, ybcd_ref, mod_ref, wout_ref, gffn_ref, wqt_ref,
                k1_ref, k2_ref, zo_ref, ft_ref, c1_ref, e1_ref, r2_ref, e2_ref, qt_scr):
    ya = ((hf_ref[...] + hb_ref[...]) * gag_ref[...]).astype(BF16)
    y = jnp.concatenate([ya, ybcd_ref[...]], axis=1)
    o = jnp.dot(y, wout_ref[...], preferred_element_type=F32)
    z = z_ref[...] + mod_ref[2:3, :] * o
    zo_ref[...] = z
    f = _rms(z, gffn_ref[...]) * (1.0 + mod_ref[4:5, :]) + mod_ref[3:4, :]
    ft = f.T.astype(BF16)
    ft_ref[...] = ft
    qt_scr[...] = jnp.dot(wqt_ref[...], ft, preferred_element_type=F32)
    lanes = 128

    def head_body(h, carry):
        base = pl.multiple_of(h * (2 * PEER_HALF), 2 * PEER_HALF)
        q1 = qt_scr[pl.ds(base, PEER_HALF), :].astype(BF16)
        q2 = qt_scr[pl.ds(base + PEER_HALF, PEER_HALF), :].astype(BF16)
        s1_all = jnp.dot(k1_ref[...], q1, preferred_element_type=F32)
        s2_all = jnp.dot(k2_ref[...], q2, preferred_element_type=F32)
        for lt in range(MID_TILE // lanes):
            sl = slice(lt * lanes, (lt + 1) * lanes)
            s1 = s1_all[:, sl]
            s2 = s2_all[:, sl]
            v1r = _top16_of_keys(s1)
            v2r = _top16_of_keys(s2)
            r2 = _rank16(s2, v2r)
            v1 = jnp.concatenate(v1r, axis=0)
            v2 = jnp.concatenate(v2r, axis=0)
            tops = _top16_of_columns([v1 + v2r[b] for b in range(PEER_TOPK)])
            m = tops[0]
            tau = tops[PEER_TOPK - 1]
            zsum = jnp.zeros_like(m)
            for row in tops:
                zsum = zsum + jnp.exp(row - m)
            theta = jnp.full(v2.shape, -NEG_BIG, F32)
            for a in range(PEER_TOPK):
                theta = jnp.minimum(theta, jnp.where(v1r[a] + v2 >= tau, v1r[a], -NEG_BIG))
            th = [theta[b:b + 1, :] for b in range(8)]
            m4 = s1 >= th[3]
            m2 = s1 >= jnp.where(m4, th[5], th[1])
            m1 = s1 >= jnp.where(m4, jnp.where(m2, th[6], th[4]), jnp.where(m2, th[2], th[0]))
            c1 = jnp.where(m4, 4.0, 0.0) + jnp.where(m2, 2.0, 0.0) + jnp.where(m1, 1.0, 0.0)
            c1 = jnp.where(s1 >= th[7], 8.0, c1)
            n_hi = jnp.sum(jnp.where(theta[8:16, :] < 1.0e38, 1.0, 0.0), axis=0, keepdims=True)
            c1 = c1 + jnp.where(s1 >= v1r[0], n_hi, 0.0)
            c1_ref[h, :, sl] = c1
            e1_ref[h, :, sl] = jnp.exp(s1 - v1r[0]) * (1.0 / zsum)
            r2_ref[h, :, sl] = r2.astype(BF16)
            e2_ref[h, :, sl] = jnp.exp(s2 - v2r[0]).astype(BF16)
        return carry

    def head_group(hg, carry):
        for j in range(MID_HEADS_PER_ITER):
            head_body(MID_HEADS_PER_ITER * hg + j, carry)
        return carry

    lax.fori_loop(0, PEER_HEADS // MID_HEADS_PER_ITER, head_group, 0)


def _mid_call(z, hf, hb, gag, ybcd, mods, lw, layer, n_ctx_tok, seq, n_batch, skip_ctx):
    n, d = z.shape
    tt = MID_TILE
    nct = n_ctx_tok // tt
    tps = seq // tt
    tile_off = nct if skip_ctx else 0
    nt = n // tt - tile_off
    nq = PEER_HEADS * 2 * PEER_HALF

    def mrow(i):
        j = i + tile_off
        return jnp.where(j < nct, n_batch, (j - nct) // tps)

    full = lambda shape: pl.BlockSpec(shape, lambda i: (0,) * len(shape))
    tok = lambda w: pl.BlockSpec((tt, w), lambda i: (i + tile_off, 0))
    in_specs = [tok(d), tok(W_GROUP), tok(W_GROUP), tok(W_GROUP), tok(3 * W_GROUP),
                pl.BlockSpec((None, N_MOD, d), lambda i: (mrow(i), 0, 0)),
                pl.BlockSpec((None, d, d), lambda i: (layer, 0, 0)), full((1, d)),
                pl.BlockSpec((None, nq, d), lambda i: (layer, 0, 0)),
                full((N_KEYS, PEER_HALF)), full((N_KEYS, PEER_HALF))]
    hk = lambda: pl.BlockSpec((PEER_HEADS, N_KEYS, tt), lambda i: (0, 0, i + tile_off))
    out_specs = [tok(d), pl.BlockSpec((d, tt), lambda i: (0, i + tile_off)), hk(), hk(), hk(), hk()]
    hk_shape = lambda dt: jax.ShapeDtypeStruct((PEER_HEADS, N_KEYS, n), dt)
    out_shape = [jax.ShapeDtypeStruct((n, d), F32), jax.ShapeDtypeStruct((d, n), BF16),
                 hk_shape(F32), hk_shape(F32), hk_shape(BF16), hk_shape(BF16)]
    return pl.pallas_call(
        _mid_kernel,
        grid=(nt,),
        in_specs=in_specs, out_specs=out_specs, out_shape=out_shape,
        scratch_shapes=[pltpu.VMEM((nq, tt), F32)],
        compiler_params=pltpu.CompilerParams(vmem_limit_bytes=VMEM_LIMIT),
        name="mid",
    )(z, hf, hb, gag, ybcd, mods, lw["w_out"], lw["g_ffn"], lw["wq_t"], lw["k1"], lw["k2"])


def _peer_kernel(final, ft_ref, c1_ref, e1_ref, r2_ref, e2_ref, u_ref, vt_ref, z_ref, mod_ref,
                 gfin_ref, zo_ref, acc_ref, a_scr, h_scr0, h_scr1, w_scr0, w_scr1):
    c = pl.program_id(1)
    tl = PEER_LANES
    sub = PEER_SUB
    pk = BF16_ROWS
    n_sub = PEER_ECHUNK // sub
    h_scr = (h_scr0, h_scr1)
    w_scr = (w_scr0, w_scr1)

    @pl.when(c == 0)
    def _():
        acc_ref[...] = jnp.zeros_like(acc_ref)

    def pre_act(k):
        h_scr[(k // 2) % 2][...] = jnp.dot(u_ref[k * sub:(k + 2) * sub, :], ft_ref[...],
                                           preferred_element_type=F32)

    def gates(k):
        for il in range(sub // N_KEYS):
            i_loc = k * (sub // N_KEYS) + il
            for lt in range(PEER_TILE // tl):
                ls = slice(lt * tl, (lt + 1) * tl)
                w = jnp.zeros((N_KEYS // pk, pk, tl), BF16)
                for h in range(PEER_HEADS):
                    c1 = jnp.broadcast_to(c1_ref[h, i_loc:i_loc + 1, ls], (pk, tl)).astype(BF16)
                    e1 = jnp.broadcast_to(e1_ref[h, i_loc:i_loc + 1, ls], (pk, tl)).astype(BF16)
                    sel = jnp.where(r2_ref[h, :, :, ls] < c1[None], e2_ref[h, :, :, ls], 0.0)
                    w = w + sel * e1[None]
                w_scr[k % 2][il * N_KEYS:(il + 1) * N_KEYS, ls] = w.reshape(N_KEYS, tl)

    def activate(k):
        half = k % 2
        g = _gelu_sigmoid(h_scr[(k // 2) % 2][half * sub:(half + 1) * sub, :].astype(BF16))
        a_scr[k * sub:(k + 1) * sub, :] = g * w_scr[k % 2][...]

    def project(k):
        acc_ref[...] += jnp.dot(vt_ref[:, k * sub:(k + 1) * sub], a_scr[k * sub:(k + 1) * sub, :],
                                preferred_element_type=F32)

    pre_act(0)
    gates(0)
    for k in range(n_sub):
        if k % 2 == 0 and k + 2 < n_sub:
            pre_act(k + 2)
        if k >= 1:
            project(k - 1)
        activate(k)
        if k + 1 < n_sub:
            gates(k + 1)
    project(n_sub - 1)

    @pl.when(c == pl.num_programs(1) - 1)
    def _():
        z = z_ref[...] + mod_ref[5:6, :] * acc_ref[...].T
        zo_ref[...] = _rms(z, gfin_ref[...]) if final else z


def _peer_call(z, ft, c1, e1, r2, e2, mods, lw, layer, g_final, n_ctx_tok, seq, n_batch, final):
    n, d = z.shape
    t = PEER_TILE
    ec = PEER_ECHUNK
    n_exp = lw["u"].shape[1]
    assert n_exp % ec == 0 and (ec // PEER_SUB) % 2 == 0
    nct = n_ctx_tok // t
    tps = seq // t
    ic = ec // N_KEYS
    pk = BF16_ROWS
    off = nct if final else 0
    r2 = r2.reshape(PEER_HEADS, N_KEYS // pk, pk, n)
    e2 = e2.reshape(PEER_HEADS, N_KEYS // pk, pk, n)

    def mrow(i):
        j = i + off
        return jnp.where(j < nct, n_batch, (j - nct) // tps)

    in_specs = [
        pl.BlockSpec((d, t), lambda i, c: (0, i + off)),
        pl.BlockSpec((PEER_HEADS, ic, t), lambda i, c: (0, c, i + off)),
        pl.BlockSpec((PEER_HEADS, ic, t), lambda i, c: (0, c, i + off)),
        pl.BlockSpec((PEER_HEADS, N_KEYS // pk, pk, t), lambda i, c: (0, 0, 0, i + off)),
        pl.BlockSpec((PEER_HEADS, N_KEYS // pk, pk, t), lambda i, c: (0, 0, 0, i + off)),
        pl.BlockSpec((None, ec, d), lambda i, c: (layer, c, 0)),
        pl.BlockSpec((None, d, ec), lambda i, c: (layer, 0, c)),
        pl.BlockSpec((t, d), lambda i, c: (i + off, 0)),
        pl.BlockSpec((None, N_MOD, d), lambda i, c: (mrow(i), 0, 0)),
        pl.BlockSpec((1, d), lambda i, c: (0, 0)),
    ]
    return pl.pallas_call(
        functools.partial(_peer_kernel, final),
        grid=(n // t - off, n_exp // ec),
        in_specs=in_specs,
        out_specs=pl.BlockSpec((t, d), lambda i, c: (i, 0)),
        out_shape=jax.ShapeDtypeStruct((n - off * t, d), F32),
        scratch_shapes=[pltpu.VMEM((d, t), F32), pltpu.VMEM((ec, t), BF16)]
        + [pltpu.VMEM((2 * PEER_SUB, t), F32)] * 2 + [pltpu.VMEM((PEER_SUB, t), BF16)] * 2,
        compiler_params=pltpu.CompilerParams(vmem_limit_bytes=VMEM_LIMIT),
        name="peer",
    )(ft, c1, e1, r2, e2, lw["u"], lw["v_t"], z, mods, g_final)


def _block_diag(w):
    h, a, b = w.shape
    eye = jnp.eye(h, dtype=w.dtype)
    return (eye[:, None, :, None] * w[:, :, None, :]).reshape(h * a, h * b)


def _grid_sincos(n, dim):
    rows = n // GRID_W
    quarter = dim // 4
    omega = POS_BASE ** (-jnp.arange(quarter, dtype=F32) / quarter)

    def axis_emb(p):
        ang = p[:, None] * omega[None, :]
        return jnp.concatenate([jnp.sin(ang), jnp.cos(ang)], axis=-1)

    row_emb = jnp.repeat(axis_emb(jnp.arange(rows, dtype=F32)), GRID_W, axis=0)
    col_emb = jnp.tile(axis_emb(jnp.arange(GRID_W, dtype=F32)), (rows, 1))
    return jnp.concatenate([row_emb, col_emb], axis=-1)


def kernel(x, c, ctx, c_ctx, w_ada, b_ada, g_mix, g_ffn, w_in, w_out, conv_a_w, conv_a_b, lru_l, lru_wr, lru_br, lru_wi, lru_bi, pool_w, pool_scale, conv_c_w, sgu_g, sgu_w, sgu_b, peer_wq, peer_k1, peer_k2, peer_u, peer_v, g_final):
    bsz, seq, d = x.shape
    ctx_len = ctx.shape[1]
    depth = w_ada.shape[0]
    assert ctx_len == TOK_TILE and seq % TOK_TILE == 0 and seq % PEER_TILE == 0 and seq % MID_TILE == 0
    assert (bsz * ctx_len) % PEER_TILE == 0 and (bsz * ctx_len) % MID_TILE == 0 and bsz + 1 <= 8
    n_ctx_tok = bsz * ctx_len
    nct = n_ctx_tok // TOK_TILE
    tps = seq // TOK_TILE

    cvec = jnp.concatenate([c, c_ctx[None, :], jnp.zeros((8 - bsz - 1, d), F32)], axis=0)
    mods_all = _ada_call(cvec, w_ada, b_ada).reshape(depth, 8, N_MOD, d)

    pos_tab = jnp.concatenate([jnp.zeros((TOK_TILE, d), F32), _grid_sincos(seq, d)], axis=0)
    z = jnp.concatenate([ctx.reshape(n_ctx_tok, d), x.reshape(bsz * seq, d)], axis=0)

    stacked = {
        "w_in": w_in.astype(BF16), "w_out": w_out.astype(BF16),
        "wq_t": jnp.swapaxes(peer_wq, 1, 2).astype(BF16),
        "u": peer_u.astype(BF16), "v_t": jnp.swapaxes(peer_v, 1, 2).astype(BF16),
    }

    for l in range(depth):
        wg = jnp.stack([jnp.concatenate([_block_diag(lru_wr[l, dd]), _block_diag(lru_wi[l, dd])], axis=1)
                        for dd in range(2)]).astype(BF16)
        bg = jnp.stack([jnp.concatenate([lru_br[l, dd].reshape(1, -1), lru_bi[l, dd].reshape(1, -1)], axis=1)
                        for dd in range(2)])
        lw = {
            "g_mix": g_mix[l].reshape(1, d), "g_ffn": g_ffn[l].reshape(1, d),
            "conv_a_w": conv_a_w[l], "conv_a_b": conv_a_b[l].reshape(1, -1),
            "conv_c_w": conv_c_w[l],
            "pool_w": _block_diag(pool_w[l]).astype(BF16), "pool_scale": pool_scale[l].reshape(1, -1),
            "sgu_g": sgu_g[l].reshape(1, -1),
            "sgu_w": jnp.transpose(sgu_w[l], (1, 0, 2)).reshape(SGU_CHUNK, SGU_HEADS * SGU_CHUNK).astype(BF16),
            "sgu_b": jnp.repeat(sgu_b[l].T, W_GROUP // SGU_HEADS, axis=1),
            "lru_wg": wg, "lru_bg": bg, "lru_l": lru_l[l],
            "k1": peer_k1[l].astype(BF16), "k2": peer_k2[l].astype(BF16),
        }
        lw.update(stacked)
        mods = mods_all[l]
        xa, gag, ybcd = _mix_in_call(z, pos_tab, mods, lw, l, nct, tps, bsz)
        hf, hb = _scan_call(xa, lw, nct, tps, bsz)
        last = l == depth - 1
        z, ft, c1, e1, r2, e2 = _mid_call(z, hf, hb, gag, ybcd, mods, lw, l, n_ctx_tok, seq, bsz, last)
        z = _peer_call(z, ft, c1, e1, r2, e2, mods, lw, l, g_final.reshape(1, d), n_ctx_tok, seq,
                       bsz, last)

    return z.reshape(bsz, seq, d)
```

```python
import functools

import jax
import jax.numpy as jnp
from jax import lax
from jax.experimental import pallas as pl
from jax.experimental.pallas import tpu as pltpu

F32 = jnp.float32
BF16 = jnp.bfloat16

W_GROUP = 256
LRU_HEADS = 4
LRU_C = 8.0
POOL_WINDOWS = (2, 4, 8, 16)
SGU_CHUNK = 128
SGU_HEADS = 4
PROJ_DIM = 8 * W_GROUP
N_KEYS = 128
PEER_HEADS = 8
PEER_HALF = 128
PEER_TOPK = 16
N_MOD = 6
EPS = 1e-6
POS_BASE = 10000.0
GRID_W = 64

TOK_TILE = 256
MID_TILE = 512
MID_HEADS_PER_ITER = 4
HALO = 8
PEER_TILE = 512
PEER_ECHUNK = 2048
PEER_SUB = 512
PEER_LANES = 256
BF16_ROWS = 16
NEG_BIG = -3.0e38
VMEM_LIMIT = 56 * 1024 * 1024


def _gelu(x):
    return 0.5 * x * (1.0 + jnp.tanh(0.7978845608028654 * (x + 0.044715 * x * x * x)))


def _gelu_sigmoid(x):
    k1 = -2.0 * 0.7978845608028654 * 1.4426950408889634
    return x / (1.0 + jnp.exp2(x * (x * x * (k1 * 0.044715) + k1)))


def _rms(x, g):
    return x * lax.rsqrt(jnp.mean(x * x, axis=-1, keepdims=True) + EPS) * g


def _ada_kernel(c_ref, w_ref, b_ref, o_ref):
    cv = c_ref[...]
    s = cv * jax.nn.sigmoid(cv)
    o_ref[...] = jnp.dot(s.astype(BF16), w_ref[...].astype(BF16),
                         preferred_element_type=F32) + b_ref[...]


def _ada_call(cvec, w_ada, b_ada):
    depth, d, nm = w_ada.shape
    tn = 1536
    return pl.pallas_call(
        _ada_kernel,
        grid=(depth, nm // tn),
        in_specs=[pl.BlockSpec((8, d), lambda l, j: (0, 0)),
                  pl.BlockSpec((None, d, tn), lambda l, j: (l, 0, j)),
                  pl.BlockSpec((None, 1, tn), lambda l, j: (l, 0, j))],
        out_specs=pl.BlockSpec((None, 8, tn), lambda l, j: (l, 0, j)),
        out_shape=jax.ShapeDtypeStruct((depth, 8, nm), F32),
        compiler_params=pltpu.CompilerParams(vmem_limit_bytes=VMEM_LIMIT),
        name="ada",
    )(cvec, w_ada, b_ada.reshape(depth, 1, nm))


def _mix_in_kernel(nct, tps, zp_ref, z_ref, zn_ref, pp_ref, p_ref, pn_ref, mod_ref, gmix_ref,
                   win_ref, cwa_ref, cba_ref, cwc_ref, poolw_ref, pscale_ref, sgug_ref,
                   sguw_ref, sgub_ref, xa_ref, gag_ref, ybcd_ref,
                   ext_scr, sa_scr, sb_scr, sc_scr, sd_scr):
    tt = TOK_TILE
    i = pl.program_id(0)
    is_ctx = i < nct
    pos_tile = jnp.where(is_ctx, 0, (i - nct) % tps)
    seq_tiles = jnp.where(is_ctx, 1, tps)
    first = pos_tile == 0
    last = pos_tile == seq_tiles - 1
    t0 = pos_tile * tt
    n_seq = seq_tiles * tt

    z_ext = jnp.concatenate([zp_ref[...], z_ref[...], zn_ref[...]], axis=0)
    pos_ext = jnp.concatenate([pp_ref[...], p_ref[...], pn_ref[...]], axis=0)
    a = _rms(z_ext, gmix_ref[...])
    a = a * (1.0 + mod_ref[1:2, :]) + mod_ref[0:1, :] + pos_ext
    row = lax.broadcasted_iota(jnp.int32, (tt + 2 * HALO, 1), 0)
    keep_prev = jnp.where(first, 0.0, 1.0)
    keep_next = jnp.where(last, 0.0, 1.0)
    keep = jnp.where(row < HALO, keep_prev, jnp.where(row >= tt + HALO, keep_next, 1.0))
    a = a * keep
    p = jnp.dot(a.astype(BF16), win_ref[...], preferred_element_type=F32)

    zeros8 = jnp.zeros((HALO, PROJ_DIM), F32)
    ext_scr[0:HALO, :] = zeros8
    ext_scr[tt + 3 * HALO:tt + 4 * HALO, :] = zeros8
    ext_scr[HALO:tt + 3 * HALO, :] = p
    m0 = 2 * HALO
    ne = tt + 2 * HALO

    cwa = cwa_ref[...]
    xa = cba_ref[...]
    for k in range(4):
        xa = xa + cwa[k:k + 1, :] * ext_scr[pl.ds(m0 - 2 + k, tt), 0:W_GROUP]
    xa_ref[...] = xa
    gag_ref[...] = _gelu(ext_scr[pl.ds(m0, tt), W_GROUP:2 * W_GROUP])

    z8 = jnp.zeros((HALO, W_GROUP), F32)
    for scr in (sa_scr, sb_scr, sc_scr, sd_scr):
        scr[0:HALO, :] = z8
        scr[tt + 3 * HALO:tt + 4 * HALO, :] = z8
    c0 = 2 * W_GROUP
    sa_scr[HALO:HALO + ne, :] = (ext_scr[pl.ds(HALO - 1, ne), c0:c0 + W_GROUP]
                                 + ext_scr[pl.ds(HALO, ne), c0:c0 + W_GROUP])
    sb_scr[HALO:HALO + ne, :] = sa_scr[pl.ds(HALO - 1, ne), :] + sa_scr[pl.ds(HALO + 1, ne), :]
    sc_scr[HALO:HALO + ne, :] = sb_scr[pl.ds(HALO - 2, ne), :] + sb_scr[pl.ds(HALO + 2, ne), :]
    p16 = sc_scr[pl.ds(m0 - 4, tt), :] + sc_scr[pl.ds(m0 + 4, tt), :]
    p2 = sa_scr[pl.ds(m0, tt), :]
    p4 = sb_scr[pl.ds(m0, tt), :]
    p8 = sc_scr[pl.ds(m0, tt), :]
    grp = lax.broadcasted_iota(jnp.int32, (tt, W_GROUP), 1) // (W_GROUP // len(POOL_WINDOWS))
    tpos = t0 + lax.broadcasted_iota(jnp.int32, (tt, W_GROUP), 0)
    sums = jnp.where(grp == 0, p2, jnp.where(grp == 1, p4, jnp.where(grp == 2, p8, p16)))
    half = jnp.where(grp == 0, 1, jnp.where(grp == 1, 2, jnp.where(grp == 2, 4, 8)))
    cnt = jnp.minimum(tpos + half, n_seq) - jnp.maximum(tpos - half, 0)
    bx = ext_scr[pl.ds(m0, tt), c0:c0 + W_GROUP]
    dpool = sums / cnt.astype(F32) - bx
    y_b = jnp.dot(dpool.astype(BF16), poolw_ref[...], preferred_element_type=F32) * pscale_ref[...]
    ybcd_ref[:, 0:W_GROUP] = y_b.astype(BF16)

    c_cb, c_cc, c_ch = 3 * W_GROUP, 4 * W_GROUP, 5 * W_GROUP
    sd_scr[HALO:HALO + ne, :] = (ext_scr[pl.ds(HALO, ne), c_cc:c_cc + W_GROUP]
                                 * ext_scr[pl.ds(HALO, ne), c_ch:c_ch + W_GROUP])
    cwc = cwc_ref[...]
    conv = jnp.zeros((tt, W_GROUP), F32)
    for k in range(3):
        conv = conv + cwc[k:k + 1, :] * sd_scr[pl.ds(m0 - 1 + k, tt), :]
    y_c = ext_scr[pl.ds(m0, tt), c_cb:c_cb + W_GROUP] * conv
    ybcd_ref[:, W_GROUP:2 * W_GROUP] = y_c.astype(BF16)

    c_du, c_dv = 6 * W_GROUP, 7 * W_GROUP
    u = _gelu(ext_scr[pl.ds(m0, tt), c_du:c_du + W_GROUP])
    gv = _gelu(ext_scr[pl.ds(m0, tt), c_dv:c_dv + W_GROUP])
    xc = gv - jnp.mean(gv, axis=-1, keepdims=True)
    v = xc * lax.rsqrt(jnp.mean(xc * xc, axis=-1, keepdims=True) + EPS) * sgug_ref[...]
    head = lax.broadcasted_iota(jnp.int32, (SGU_CHUNK, W_GROUP), 1) // (W_GROUP // SGU_HEADS)
    for c in range(tt // SGU_CHUNK):
        vc = v[c * SGU_CHUNK:(c + 1) * SGU_CHUNK, :]
        stack = jnp.concatenate([jnp.where(head == h, vc, 0.0) for h in range(SGU_HEADS)], axis=0)
        mixed = jnp.dot(sguw_ref[...], stack.astype(BF16), preferred_element_type=F32) + sgub_ref[...]
        y_d = u[c * SGU_CHUNK:(c + 1) * SGU_CHUNK, :] * mixed
        ybcd_ref[c * SGU_CHUNK:(c + 1) * SGU_CHUNK, 2 * W_GROUP:3 * W_GROUP] = y_d.astype(BF16)


def _mix_in_call(z, pos_tab, mods, lw, layer, nct, tps, n_batch):
    n, d = z.shape
    tt = TOK_TILE
    nt = n // tt
    hb = tt // HALO
    n_pos = pos_tab.shape[0]

    def mrow(i):
        return jnp.where(i < nct, n_batch, (i - nct) // tps)

    def pblk(i):
        return jnp.where(i < nct, 0, 1 + (i - nct) % tps)

    full = lambda shape: pl.BlockSpec(shape, lambda i: (0,) * len(shape))
    in_specs = [
        pl.BlockSpec((HALO, d), lambda i: (jnp.maximum(i * hb - 1, 0), 0)),
        pl.BlockSpec((tt, d), lambda i: (i, 0)),
        pl.BlockSpec((HALO, d), lambda i: (jnp.minimum((i + 1) * hb, n // HALO - 1), 0)),
        pl.BlockSpec((HALO, d), lambda i: (jnp.maximum(pblk(i) * hb - 1, 0), 0)),
        pl.BlockSpec((tt, d), lambda i: (pblk(i), 0)),
        pl.BlockSpec((HALO, d), lambda i: (jnp.minimum((pblk(i) + 1) * hb, n_pos // HALO - 1), 0)),
        pl.BlockSpec((None, N_MOD, d), lambda i: (mrow(i), 0, 0)),
        full((1, d)),
        pl.BlockSpec((None, d, PROJ_DIM), lambda i: (layer, 0, 0)),
        full((4, W_GROUP)), full((1, W_GROUP)), full((3, W_GROUP)),
        full((W_GROUP, W_GROUP)), full((1, W_GROUP)), full((1, W_GROUP)),
        full((SGU_CHUNK, SGU_HEADS * SGU_CHUNK)), full((SGU_CHUNK, W_GROUP)),
    ]
    out_specs = [pl.BlockSpec((tt, W_GROUP), lambda i: (i, 0)),
                 pl.BlockSpec((tt, W_GROUP), lambda i: (i, 0)),
                 pl.BlockSpec((tt, 3 * W_GROUP), lambda i: (i, 0))]
    out_shape = [jax.ShapeDtypeStruct((n, W_GROUP), F32),
                 jax.ShapeDtypeStruct((n, W_GROUP), F32),
                 jax.ShapeDtypeStruct((n, 3 * W_GROUP), BF16)]
    er = tt + 4 * HALO
    return pl.pallas_call(
        functools.partial(_mix_in_kernel, nct, tps),
        grid=(nt,),
        in_specs=in_specs, out_specs=out_specs, out_shape=out_shape,
        scratch_shapes=[pltpu.VMEM((er, PROJ_DIM), F32)] + [pltpu.VMEM((er, W_GROUP), F32)] * 4,
        compiler_params=pltpu.CompilerParams(vmem_limit_bytes=VMEM_LIMIT),
        name="mix_in",
    )(z, z, z, pos_tab, pos_tab, pos_tab, mods, lw["g_mix"], lw["w_in"], lw["conv_a_w"],
      lw["conv_a_b"], lw["conv_c_w"], lw["pool_w"], lw["pool_scale"], lw["sgu_g"],
      lw["sgu_w"], lw["sgu_b"])


def _chunk_scan(a, b, reverse):
    t = a.shape[0]
    row = lax.broadcasted_iota(jnp.int32, a.shape, 0)
    d = 1
    while d < t:
        shift = t - d if reverse else d
        ok = (row < t - d) if reverse else (row >= d)
        a_s = jnp.where(ok, pltpu.roll(a, shift, axis=0), 1.0)
        b_s = jnp.where(ok, pltpu.roll(b, shift, axis=0), 0.0)
        b = b + a * b_s
        a = a * a_s
        d *= 2
    return a, b


def _scan_kernel(xf_ref, xb_ref, wg_ref, bg_ref, lam_ref, hf_ref, hb_ref, carry_ref):
    tt = TOK_TILE
    s = pl.program_id(1)

    @pl.when(s == 0)
    def _():
        carry_ref[...] = jnp.zeros_like(carry_ref)

    for d, (x_ref, o_ref) in enumerate(((xf_ref, hf_ref), (xb_ref, hb_ref))):
        xa = x_ref[...]
        g = jnp.dot(xa.astype(BF16), wg_ref[d], preferred_element_type=F32) + bg_ref[d]
        r = jax.nn.sigmoid(g[:, 0:W_GROUP])
        gi = jax.nn.sigmoid(g[:, W_GROUP:2 * W_GROUP])
        neg_lam = -lam_ref[d:d + 1, :]
        softplus = jnp.maximum(neg_lam, 0.0) + jnp.log1p(jnp.exp(-jnp.abs(neg_lam)))
        log_a = (-LRU_C) * r * softplus
        a = jnp.exp(log_a)
        b = jnp.sqrt(1.0 - a * a) * (gi * xa)
        a_cum, h0 = _chunk_scan(a, b, reverse=(d == 1))
        h = h0 + a_cum * carry_ref[d:d + 1, :]
        o_ref[...] = h
        edge = 0 if d == 1 else tt - 1
        carry_ref[d:d + 1, :] = h[edge:edge + 1, :]


def _scan_call(xa, lw, nct, tps, n_batch):
    n, c = xa.shape
    tt = TOK_TILE
    fwd = lambda b, s: (jnp.where(s == 0, b, nct + b * tps + s - 1), 0)
    bwd = lambda b, s: (jnp.where(s == 0, b, nct + b * tps + tps - s), 0)
    full = lambda shape: pl.BlockSpec(shape, lambda b, s: (0,) * len(shape))
    return pl.pallas_call(
        _scan_kernel,
        grid=(n_batch, tps + 1),
        in_specs=[pl.BlockSpec((tt, c), fwd), pl.BlockSpec((tt, c), bwd),
                  full((2, c, 2 * c)), full((2, 1, 2 * c)), full((2, c))],
        out_specs=[pl.BlockSpec((tt, c), fwd), pl.BlockSpec((tt, c), bwd)],
        out_shape=[jax.ShapeDtypeStruct((n, c), F32)] * 2,
        scratch_shapes=[pltpu.VMEM((2, c), F32)],
        compiler_params=pltpu.CompilerParams(vmem_limit_bytes=VMEM_LIMIT),
        name="lru_scan",
    )(xa, xa, lw["lru_wg"], lw["lru_bg"], lw["lru_l"])


def _sort16_network():
    def merge(lo, hi, r):
        step = r * 2
        if step < hi - lo:
            yield from merge(lo, hi, step)
            yield from merge(lo + r, hi, step)
            yield from [(i, i + r) for i in range(lo + r, hi - r, step)]
        else:
            yield (lo, lo + r)

    def sort(lo, hi):
        if hi - lo >= 1:
            mid = lo + (hi - lo) // 2
            yield from sort(lo, mid)
            yield from sort(mid + 1, hi)
            yield from merge(lo, hi, 1)

    return tuple(sort(0, PEER_TOPK - 1))


_SORT16 = _sort16_network()


def _top16_of_columns(cols):
    cols = list(cols)
    rows = []
    for r in range(PEER_TOPK):
        mx = jnp.max(cols[0], axis=0, keepdims=True)
        rows.append(mx)
        if r + 1 < PEER_TOPK:
            hit = cols[0] == mx
            depth = PEER_TOPK - r
            for d in range(depth - 1):
                cols[d] = jnp.where(hit, cols[d + 1], cols[d])
            cols[depth - 1] = jnp.where(hit, NEG_BIG, cols[depth - 1])
    return rows


def _top16_of_keys(s):
    cols = [s[8 * g:8 * g + 8, :] for g in range(N_KEYS // 8)]
    for a, b in _SORT16:
        cols[a], cols[b] = jnp.maximum(cols[a], cols[b]), jnp.minimum(cols[a], cols[b])
    return _top16_of_columns(cols)


def _rank16(s, v):
    m8 = s >= v[7]
    m4 = s >= jnp.where(m8, v[3], v[11])
    m2 = s >= jnp.where(m8, jnp.where(m4, v[1], v[5]), jnp.where(m4, v[9], v[13]))
    m1 = s >= jnp.where(m8, jnp.where(m4, jnp.where(m2, v[0], v[2]), jnp.where(m2, v[4], v[6])),
                        jnp.where(m4, jnp.where(m2, v[8], v[10]), jnp.where(m2, v[12], v[14])))
    rank = (jnp.where(m8, 0.0, 8.0) + jnp.where(m4, 0.0, 4.0)
            + jnp.where(m2, 0.0, 2.0) + jnp.where(m1, 0.0, 1.0))
    return jnp.where(s >= v[15], rank, float(PEER_TOPK))


def _mid_kernel(z_ref, hf_ref, hb_ref, gag_ref, ybcd_ref, mod_ref, wout_ref, gffn_ref, wqt_ref,
                k1_ref, k2_ref, zo_ref, ft_ref, c1_ref, e1_ref, r2_ref, e2_ref, qt_scr):
    ya = ((hf_ref[...] + hb_ref[...]) * gag_ref[...]).astype(BF16)
    y = jnp.concatenate([ya, ybcd_ref[...]], axis=1)
    o = jnp.dot(y, wout_ref[...], preferred_element_type=F32)
    z = z_ref[...] + mod_ref[2:3, :] * o
    zo_ref[...] = z
    f = _rms(z, gffn_ref[...]) * (1.0 + mod_ref[4:5, :]) + mod_ref[3:4, :]
    ft = f.T.astype(BF16)
    ft_ref[...] = ft
    qt_scr[...] = jnp.dot(wqt_ref[...], ft, preferred_element_type=F32)
    lanes = 128

    def head_body(h, carry):
        base = pl.multiple_of(h * (2 * PEER_HALF), 2 * PEER_HALF)
        q1 = qt_scr[pl.ds(base, PEER_HALF), :].astype(BF16)
        q2 = qt_scr[pl.ds(base + PEER_HALF, PEER_HALF), :].astype(BF16)
        s1_all = jnp.dot(k1_ref[...], q1, preferred_element_type=F32)
        s2_all = jnp.dot(k2_ref[...], q2, preferred_element_type=F32)
        for lt in range(MID_TILE // lanes):
            sl = slice(lt * lanes, (lt + 1) * lanes)
            s1 = s1_all[:, sl]
            s2 = s2_all[:, sl]
            v1r = _top16_of_keys(s1)
            v2r = _top16_of_keys(s2)
            r2 = _rank16(s2, v2r)
            v1 = jnp.concatenate(v1r, axis=0)
            v2 = jnp.concatenate(v2r, axis=0)
            tops = _top16_of_columns([v1 + v2r[b] for b in range(PEER_TOPK)])
            m = tops[0]
            tau = tops[PEER_TOPK - 1]
            zsum = jnp.zeros_like(m)
            for row in tops:
                zsum = zsum + jnp.exp(row - m)
            theta = jnp.full(v2.shape, -NEG_BIG, F32)
            for a in range(PEER_TOPK):
                theta = jnp.minimum(theta, jnp.where(v1r[a] + v2 >= tau, v1r[a], -NEG_BIG))
            th = [theta[b:b + 1, :] for b in range(8)]
            m4 = s1 >= th[3]
            m2 = s1 >= jnp.where(m4, th[5], th[1])
            m1 = s1 >= jnp.where(m4, jnp.where(m2, th[6], th[4]), jnp.where(m2, th[2], th[0]))
            c1 = jnp.where(m4, 4.0, 0.0) + jnp.where(m2, 2.0, 0.0) + jnp.where(m1, 1.0, 0.0)
            c1 = jnp.where(s1 >= th[7], 8.0, c1)
            n_hi = jnp.sum(jnp.where(theta[8:16, :] < 1.0e38, 1.0, 0.0), axis=0, keepdims=True)
            c1 = c1 + jnp.where(s1 >= v1r[0], n_hi, 0.0)
            c1_ref[h, :, sl] = c1
            e1_ref[h, :, sl] = jnp.exp(s1 - v1r[0]) * (1.0 / zsum)
            r2_ref[h, :, sl] = r2.astype(BF16)
            e2_ref[h, :, sl] = jnp.exp(s2 - v2r[0]).astype(BF16)
        return carry

    def head_group(hg, carry):
        for j in range(MID_HEADS_PER_ITER):
            head_body(MID_HEADS_PER_ITER * hg + j, carry)
        return carry

    lax.fori_loop(0, PEER_HEADS // MID_HEADS_PER_ITER, head_group, 0)


def _mid_call(z, hf, hb, gag, ybcd, mods, lw, layer, n_ctx_tok, seq, n_batch, skip_ctx):
    n, d = z.shape
    tt = MID_TILE
    nct = n_ctx_tok // tt
    tps = seq // tt
    tile_off = nct if skip_ctx else 0
    nt = n // tt - tile_off
    nq = PEER_HEADS * 2 * PEER_HALF

    def mrow(i):
        j = i + tile_off
        return jnp.where(j < nct, n_batch, (j - nct) // tps)

    full = lambda shape: pl.BlockSpec(shape, lambda i: (0,) * len(shape))
    tok = lambda w: pl.BlockSpec((tt, w), lambda i: (i + tile_off, 0))
    in_specs = [tok(d), tok(W_GROUP), tok(W_GROUP), tok(W_GROUP), tok(3 * W_GROUP),
                pl.BlockSpec((None, N_MOD, d), lambda i: (mrow(i), 0, 0)),
                pl.BlockSpec((None, d, d), lambda i: (layer, 0, 0)), full((1, d)),
                pl.BlockSpec((None, nq, d), lambda i: (layer, 0, 0)),
                full((N_KEYS, PEER_HALF)), full((N_KEYS, PEER_HALF))]
    hk = lambda: pl.BlockSpec((PEER_HEADS, N_KEYS, tt), lambda i: (0, 0, i + tile_off))
    out_specs = [tok(d), pl.BlockSpec((d, tt), lambda i: (0, i + tile_off)), hk(), hk(), hk(), hk()]
    hk_shape = lambda dt: jax.ShapeDtypeStruct((PEER_HEADS, N_KEYS, n), dt)
    out_shape = [jax.ShapeDtypeStruct((n, d), F32), jax.ShapeDtypeStruct((d, n), BF16),
                 hk_shape(F32), hk_shape(F32), hk_shape(BF16), hk_shape(BF16)]
    return pl.pallas_call(
        _mid_kernel,
        grid=(nt,),
        in_specs=in_specs, out_specs=out_specs, out_shape=out_shape,
        scratch_shapes=[pltpu.VMEM((nq, tt), F32)],
        compiler_params=pltpu.CompilerParams(vmem_limit_bytes=VMEM_LIMIT),
        name="mid",
    )(z, hf, hb, gag, ybcd, mods, lw["w_out"], lw["g_ffn"], lw["wq_t"], lw["k1"], lw["k2"])


def _peer_kernel(final, ft_ref, c1_ref, e1_ref, r2_ref, e2_ref, u_ref, vt_ref, z_ref, mod_ref,
                 gfin_ref, zo_ref, acc_ref, a_scr, h_scr0, h_scr1, w_scr0, w_scr1):
    c = pl.program_id(1)
    tl = PEER_LANES
    sub = PEER_SUB
    pk = BF16_ROWS
    n_sub = PEER_ECHUNK // sub
    h_scr = (h_scr0, h_scr1)
    w_scr = (w_scr0, w_scr1)

    @pl.when(c == 0)
    def _():
        acc_ref[...] = jnp.zeros_like(acc_ref)

    def pre_act(k):
        h_scr[(k // 2) % 2][...] = jnp.dot(u_ref[k * sub:(k + 2) * sub, :], ft_ref[...],
                                           preferred_element_type=F32)

    def gates(k):
        for il in range(sub // N_KEYS):
            i_loc = k * (sub // N_KEYS) + il
            for lt in range(PEER_TILE // tl):
                ls = slice(lt * tl, (lt + 1) * tl)
                w = jnp.zeros((N_KEYS // pk, pk, tl), BF16)
                for h in range(PEER_HEADS):
                    c1 = jnp.broadcast_to(c1_ref[h, i_loc:i_loc + 1, ls], (pk, tl)).astype(BF16)
                    e1 = jnp.broadcast_to(e1_ref[h, i_loc:i_loc + 1, ls], (pk, tl)).astype(BF16)
                    sel = jnp.where(r2_ref[h, :, :, ls] < c1[None], e2_ref[h, :, :, ls], 0.0)
                    w = w + sel * e1[None]
                w_scr[k % 2][il * N_KEYS:(il + 1) * N_KEYS, ls] = w.reshape(N_KEYS, tl)

    def activate(k):
        half = k % 2
        g = _gelu_sigmoid(h_scr[(k // 2) % 2][half * sub:(half + 1) * sub, :].astype(BF16))
        a_scr[k * sub:(k + 1) * sub, :] = g * w_scr[k % 2][...]

    def project(k):
        acc_ref[...] += jnp.dot(vt_ref[:, k * sub:(k + 1) * sub], a_scr[k * sub:(k + 1) * sub, :],
                                preferred_element_type=F32)

    pre_act(0)
    gates(0)
    for k in range(n_sub):
        if k % 2 == 0 and k + 2 < n_sub:
            pre_act(k + 2)
        if k >= 1:
            project(k - 1)
        activate(k)
        if k + 1 < n_sub:
            gates(k + 1)
    project(n_sub - 1)

    @pl.when(c == pl.num_programs(1) - 1)
    def _():
        z = z_ref[...] + mod_ref[5:6, :] * acc_ref[...].T
        zo_ref[...] = _rms(z, gfin_ref[...]) if final else z


def _peer_call(z, ft, c1, e1, r2, e2, mods, lw, layer, g_final, n_ctx_tok, seq, n_batch, final):
    n, d = z.shape
    t = PEER_TILE
    ec = PEER_ECHUNK
    n_exp = lw["u"].shape[1]
    assert n_exp % ec == 0 and (ec // PEER_SUB) % 2 == 0
    nct = n_ctx_tok // t
    tps = seq // t
    ic = ec // N_KEYS
    pk = BF16_ROWS
    off = nct if final else 0
    r2 = r2.reshape(PEER_HEADS, N_KEYS // pk, pk, n)
    e2 = e2.reshape(PEER_HEADS, N_KEYS // pk, pk, n)

    def mrow(i):
        j = i + off
        return jnp.where(j < nct, n_batch, (j - nct) // tps)

    in_specs = [
        pl.BlockSpec((d, t), lambda i, c: (0, i + off)),
        pl.BlockSpec((PEER_HEADS, ic, t), lambda i, c: (0, c, i + off)),
        pl.BlockSpec((PEER_HEADS, ic, t), lambda i, c: (0, c, i + off)),
        pl.BlockSpec((PEER_HEADS, N_KEYS // pk, pk, t), lambda i, c: (0, 0, 0, i + off)),
        pl.BlockSpec((PEER_HEADS, N_KEYS // pk, pk, t), lambda i, c: (0, 0, 0, i + off)),
        pl.BlockSpec((None, ec, d), lambda i, c: (layer, c, 0)),
        pl.BlockSpec((None, None, d, ec), lambda i, c: (layer, c, 0, 0)),
        pl.BlockSpec((t, d), lambda i, c: (i + off, 0)),
        pl.BlockSpec((None, N_MOD, d), lambda i, c: (mrow(i), 0, 0)),
        pl.BlockSpec((1, d), lambda i, c: (0, 0)),
    ]
    return pl.pallas_call(
        functools.partial(_peer_kernel, final),
        grid=(n // t - off, n_exp // ec),
        in_specs=in_specs,
        out_specs=pl.BlockSpec((t, d), lambda i, c: (i, 0)),
        out_shape=jax.ShapeDtypeStruct((n - off * t, d), F32),
        scratch_shapes=[pltpu.VMEM((d, t), F32), pltpu.VMEM((ec, t), BF16)]
        + [pltpu.VMEM((2 * PEER_SUB, t), F32)] * 2 + [pltpu.VMEM((PEER_SUB, t), BF16)] * 2,
        compiler_params=pltpu.CompilerParams(vmem_limit_bytes=VMEM_LIMIT),
        name="peer",
    )(ft, c1, e1, r2, e2, lw["u"], lw["v_t"], z, mods, g_final)


def _block_diag(w):
    h, a, b = w.shape
    eye = jnp.eye(h, dtype=w.dtype)
    return (eye[:, None, :, None] * w[:, :, None, :]).reshape(h * a, h * b)


def _grid_sincos(n, dim):
    rows = n // GRID_W
    quarter = dim // 4
    omega = POS_BASE ** (-jnp.arange(quarter, dtype=F32) / quarter)

    def axis_emb(p):
        ang = p[:, None] * omega[None, :]
        return jnp.concatenate([jnp.sin(ang), jnp.cos(ang)], axis=-1)

    row_emb = jnp.repeat(axis_emb(jnp.arange(rows, dtype=F32)), GRID_W, axis=0)
    col_emb = jnp.tile(axis_emb(jnp.arange(GRID_W, dtype=F32)), (rows, 1))
    return jnp.concatenate([row_emb, col_emb], axis=-1)


def kernel(x, c, ctx, c_ctx, w_ada, b_ada, g_mix, g_ffn, w_in, w_out, conv_a_w, conv_a_b, lru_l, lru_wr, lru_br, lru_wi, lru_bi, pool_w, pool_scale, conv_c_w, sgu_g, sgu_w, sgu_b, peer_wq, peer_k1, peer_k2, peer_u, peer_v, g_final):
    bsz, seq, d = x.shape
    ctx_len = ctx.shape[1]
    depth = w_ada.shape[0]
    assert ctx_len == TOK_TILE and seq % TOK_TILE == 0 and seq % PEER_TILE == 0 and seq % MID_TILE == 0
    assert (bsz * ctx_len) % PEER_TILE == 0 and (bsz * ctx_len) % MID_TILE == 0 and bsz + 1 <= 8
    n_ctx_tok = bsz * ctx_len
    nct = n_ctx_tok // TOK_TILE
    tps = seq // TOK_TILE

    cvec = jnp.concatenate([c, c_ctx[None, :], jnp.zeros((8 - bsz - 1, d), F32)], axis=0)
    mods_all = _ada_call(cvec, w_ada, b_ada).reshape(depth, 8, N_MOD, d)

    pos_tab = jnp.concatenate([jnp.zeros((TOK_TILE, d), F32), _grid_sincos(seq, d)], axis=0)
    z = jnp.concatenate([ctx.reshape(n_ctx_tok, d), x.reshape(bsz * seq, d)], axis=0)

    stacked = {
        "w_in": w_in.astype(BF16), "w_out": w_out.astype(BF16),
        "wq_t": jnp.swapaxes(peer_wq, 1, 2).astype(BF16),
        "u": peer_u.astype(BF16),
        "v_t": jnp.transpose(peer_v.reshape(depth, -1, PEER_ECHUNK, d), (0, 1, 3, 2)).astype(BF16),
    }

    for l in range(depth):
        wg = jnp.stack([jnp.concatenate([_block_diag(lru_wr[l, dd]), _block_diag(lru_wi[l, dd])], axis=1)
                        for dd in range(2)]).astype(BF16)
        bg = jnp.stack([jnp.concatenate([lru_br[l, dd].reshape(1, -1), lru_bi[l, dd].reshape(1, -1)], axis=1)
                        for dd in range(2)])
        lw = {
            "g_mix": g_mix[l].reshape(1, d), "g_ffn": g_ffn[l].reshape(1, d),
            "conv_a_w": conv_a_w[l], "conv_a_b": conv_a_b[l].reshape(1, -1),
            "conv_c_w": conv_c_w[l],
            "pool_w": _block_diag(pool_w[l]).astype(BF16), "pool_scale": pool_scale[l].reshape(1, -1),
            "sgu_g": sgu_g[l].reshape(1, -1),
            "sgu_w": jnp.transpose(sgu_w[l], (1, 0, 2)).reshape(SGU_CHUNK, SGU_HEADS * SGU_CHUNK).astype(BF16),
            "sgu_b": jnp.repeat(sgu_b[l].T, W_GROUP // SGU_HEADS, axis=1),
            "lru_wg": wg, "lru_bg": bg, "lru_l": lru_l[l],
            "k1": peer_k1[l].astype(BF16), "k2": peer_k2[l].astype(BF16),
        }
        lw.update(stacked)
        mods = mods_all[l]
        xa, gag, ybcd = _mix_in_call(z, pos_tab, mods, lw, l, nct, tps, bsz)
        hf, hb = _scan_call(xa, lw, nct, tps, bsz)
        last = l == depth - 1
        z, ft, c1, e1, r2, e2 = _mid_call(z, hf, hb, gag, ybcd, mods, lw, l, n_ctx_tok, seq, bsz, last)
        z = _peer_call(z, ft, c1, e1, r2, e2, mods, lw, l, g_final.reshape(1, d), n_ctx_tok, seq,
                       bsz, last)

    return z.reshape(bsz, seq, d)
```

```python
import functools

import jax
import jax.numpy as jnp
from jax import lax
from jax.experimental import pallas as pl
from jax.experimental.pallas import tpu as pltpu

F32 = jnp.float32
BF16 = jnp.bfloat16

W_GROUP = 256
LRU_HEADS = 4
LRU_C = 8.0
POOL_WINDOWS = (2, 4, 8, 16)
SGU_CHUNK = 128
SGU_HEADS = 4
PROJ_DIM = 8 * W_GROUP
N_KEYS = 128
PEER_HEADS = 8
PEER_HALF = 128
PEER_TOPK = 16
N_MOD = 6
EPS = 1e-6
POS_BASE = 10000.0
GRID_W = 64

TOK_TILE = 256
MID_TILE = 512
MID_HEADS_PER_ITER = 4
HALO = 8
PEER_TILE = 512
PEER_ECHUNK = 2048
PEER_SUB = 512
PEER_LANES = 256
PEER_BUFS = 3
BF16_ROWS = 16
NEG_BIG = -3.0e38
VMEM_LIMIT = 56 * 1024 * 1024


def _gelu(x):
    return 0.5 * x * (1.0 + jnp.tanh(0.7978845608028654 * (x + 0.044715 * x * x * x)))


def _gelu_sigmoid(x):
    k1 = -2.0 * 0.7978845608028654 * 1.4426950408889634
    return x / (1.0 + jnp.exp2(x * (x * x * (k1 * 0.044715) + k1)))


def _rms(x, g):
    return x * lax.rsqrt(jnp.mean(x * x, axis=-1, keepdims=True) + EPS) * g


def _ada_kernel(c_ref, w_ref, b_ref, o_ref):
    cv = c_ref[...]
    s = cv * jax.nn.sigmoid(cv)
    o_ref[...] = jnp.dot(s.astype(BF16), w_ref[...].astype(BF16),
                         preferred_element_type=F32) + b_ref[...]


def _ada_call(cvec, w_ada, b_ada):
    depth, d, nm = w_ada.shape
    tn = 1536
    return pl.pallas_call(
        _ada_kernel,
        grid=(depth, nm // tn),
        in_specs=[pl.BlockSpec((8, d), lambda l, j: (0, 0)),
                  pl.BlockSpec((None, d, tn), lambda l, j: (l, 0, j)),
                  pl.BlockSpec((None, 1, tn), lambda l, j: (l, 0, j))],
        out_specs=pl.BlockSpec((None, 8, tn), lambda l, j: (l, 0, j)),
        out_shape=jax.ShapeDtypeStruct((depth, 8, nm), F32),
        compiler_params=pltpu.CompilerParams(vmem_limit_bytes=VMEM_LIMIT),
        name="ada",
    )(cvec, w_ada, b_ada.reshape(depth, 1, nm))


def _mix_in_kernel(nct, tps, zp_ref, z_ref, zn_ref, pp_ref, p_ref, pn_ref, mod_ref, gmix_ref,
                   win_ref, cwa_ref, cba_ref, cwc_ref, poolw_ref, pscale_ref, sgug_ref,
                   sguw_ref, sgub_ref, xa_ref, gag_ref, ybcd_ref,
                   ext_scr, sa_scr, sb_scr, sc_scr, sd_scr):
    tt = TOK_TILE
    i = pl.program_id(0)
    is_ctx = i < nct
    pos_tile = jnp.where(is_ctx, 0, (i - nct) % tps)
    seq_tiles = jnp.where(is_ctx, 1, tps)
    first = pos_tile == 0
    last = pos_tile == seq_tiles - 1
    t0 = pos_tile * tt
    n_seq = seq_tiles * tt

    z_ext = jnp.concatenate([zp_ref[...], z_ref[...], zn_ref[...]], axis=0)
    pos_ext = jnp.concatenate([pp_ref[...], p_ref[...], pn_ref[...]], axis=0)
    a = _rms(z_ext, gmix_ref[...])
    a = a * (1.0 + mod_ref[1:2, :]) + mod_ref[0:1, :] + pos_ext
    row = lax.broadcasted_iota(jnp.int32, (tt + 2 * HALO, 1), 0)
    keep_prev = jnp.where(first, 0.0, 1.0)
    keep_next = jnp.where(last, 0.0, 1.0)
    keep = jnp.where(row < HALO, keep_prev, jnp.where(row >= tt + HALO, keep_next, 1.0))
    a = a * keep
    p = jnp.dot(a.astype(BF16), win_ref[...], preferred_element_type=F32)

    zeros8 = jnp.zeros((HALO, PROJ_DIM), F32)
    ext_scr[0:HALO, :] = zeros8
    ext_scr[tt + 3 * HALO:tt + 4 * HALO, :] = zeros8
    ext_scr[HALO:tt + 3 * HALO, :] = p
    m0 = 2 * HALO
    ne = tt + 2 * HALO

    cwa = cwa_ref[...]
    xa = cba_ref[...]
    for k in range(4):
        xa = xa + cwa[k:k + 1, :] * ext_scr[pl.ds(m0 - 2 + k, tt), 0:W_GROUP]
    xa_ref[...] = xa
    gag_ref[...] = _gelu(ext_scr[pl.ds(m0, tt), W_GROUP:2 * W_GROUP])

    z8 = jnp.zeros((HALO, W_GROUP), F32)
    for scr in (sa_scr, sb_scr, sc_scr, sd_scr):
        scr[0:HALO, :] = z8
        scr[tt + 3 * HALO:tt + 4 * HALO, :] = z8
    c0 = 2 * W_GROUP
    sa_scr[HALO:HALO + ne, :] = (ext_scr[pl.ds(HALO - 1, ne), c0:c0 + W_GROUP]
                                 + ext_scr[pl.ds(HALO, ne), c0:c0 + W_GROUP])
    sb_scr[HALO:HALO + ne, :] = sa_scr[pl.ds(HALO - 1, ne), :] + sa_scr[pl.ds(HALO + 1, ne), :]
    sc_scr[HALO:HALO + ne, :] = sb_scr[pl.ds(HALO - 2, ne), :] + sb_scr[pl.ds(HALO + 2, ne), :]
    p16 = sc_scr[pl.ds(m0 - 4, tt), :] + sc_scr[pl.ds(m0 + 4, tt), :]
    p2 = sa_scr[pl.ds(m0, tt), :]
    p4 = sb_scr[pl.ds(m0, tt), :]
    p8 = sc_scr[pl.ds(m0, tt), :]
    grp = lax.broadcasted_iota(jnp.int32, (tt, W_GROUP), 1) // (W_GROUP // len(POOL_WINDOWS))
    tpos = t0 + lax.broadcasted_iota(jnp.int32, (tt, W_GROUP), 0)
    sums = jnp.where(grp == 0, p2, jnp.where(grp == 1, p4, jnp.where(grp == 2, p8, p16)))
    half = jnp.where(grp == 0, 1, jnp.where(grp == 1, 2, jnp.where(grp == 2, 4, 8)))
    cnt = jnp.minimum(tpos + half, n_seq) - jnp.maximum(tpos - half, 0)
    bx = ext_scr[pl.ds(m0, tt), c0:c0 + W_GROUP]
    dpool = sums / cnt.astype(F32) - bx
    y_b = jnp.dot(dpool.astype(BF16), poolw_ref[...], preferred_element_type=F32) * pscale_ref[...]
    ybcd_ref[:, 0:W_GROUP] = y_b.astype(BF16)

    c_cb, c_cc, c_ch = 3 * W_GROUP, 4 * W_GROUP, 5 * W_GROUP
    sd_scr[HALO:HALO + ne, :] = (ext_scr[pl.ds(HALO, ne), c_cc:c_cc + W_GROUP]
                                 * ext_scr[pl.ds(HALO, ne), c_ch:c_ch + W_GROUP])
    cwc = cwc_ref[...]
    conv = jnp.zeros((tt, W_GROUP), F32)
    for k in range(3):
        conv = conv + cwc[k:k + 1, :] * sd_scr[pl.ds(m0 - 1 + k, tt), :]
    y_c = ext_scr[pl.ds(m0, tt), c_cb:c_cb + W_GROUP] * conv
    ybcd_ref[:, W_GROUP:2 * W_GROUP] = y_c.astype(BF16)

    c_du, c_dv = 6 * W_GROUP, 7 * W_GROUP
    u = _gelu(ext_scr[pl.ds(m0, tt), c_du:c_du + W_GROUP])
    gv = _gelu(ext_scr[pl.ds(m0, tt), c_dv:c_dv + W_GROUP])
    xc = gv - jnp.mean(gv, axis=-1, keepdims=True)
    v = xc * lax.rsqrt(jnp.mean(xc * xc, axis=-1, keepdims=True) + EPS) * sgug_ref[...]
    head = lax.broadcasted_iota(jnp.int32, (SGU_CHUNK, W_GROUP), 1) // (W_GROUP // SGU_HEADS)
    for c in range(tt // SGU_CHUNK):
        vc = v[c * SGU_CHUNK:(c + 1) * SGU_CHUNK, :]
        stack = jnp.concatenate([jnp.where(head == h, vc, 0.0) for h in range(SGU_HEADS)], axis=0)
        mixed = jnp.dot(sguw_ref[...], stack.astype(BF16), preferred_element_type=F32) + sgub_ref[...]
        y_d = u[c * SGU_CHUNK:(c + 1) * SGU_CHUNK, :] * mixed
        ybcd_ref[c * SGU_CHUNK:(c + 1) * SGU_CHUNK, 2 * W_GROUP:3 * W_GROUP] = y_d.astype(BF16)


def _mix_in_call(z, pos_tab, mods, lw, layer, nct, tps, n_batch):
    n, d = z.shape
    tt = TOK_TILE
    nt = n // tt
    hb = tt // HALO
    n_pos = pos_tab.shape[0]

    def mrow(i):
        return jnp.where(i < nct, n_batch, (i - nct) // tps)

    def pblk(i):
        return jnp.where(i < nct, 0, 1 + (i - nct) % tps)

    full = lambda shape: pl.BlockSpec(shape, lambda i: (0,) * len(shape))
    in_specs = [
        pl.BlockSpec((HALO, d), lambda i: (jnp.maximum(i * hb - 1, 0), 0)),
        pl.BlockSpec((tt, d), lambda i: (i, 0)),
        pl.BlockSpec((HALO, d), lambda i: (jnp.minimum((i + 1) * hb, n // HALO - 1), 0)),
        pl.BlockSpec((HALO, d), lambda i: (jnp.maximum(pblk(i) * hb - 1, 0), 0)),
        pl.BlockSpec((tt, d), lambda i: (pblk(i), 0)),
        pl.BlockSpec((HALO, d), lambda i: (jnp.minimum((pblk(i) + 1) * hb, n_pos // HALO - 1), 0)),
        pl.BlockSpec((None, N_MOD, d), lambda i: (mrow(i), 0, 0)),
        full((1, d)),
        pl.BlockSpec((None, d, PROJ_DIM), lambda i: (layer, 0, 0)),
        full((4, W_GROUP)), full((1, W_GROUP)), full((3, W_GROUP)),
        full((W_GROUP, W_GROUP)), full((1, W_GROUP)), full((1, W_GROUP)),
        full((SGU_CHUNK, SGU_HEADS * SGU_CHUNK)), full((SGU_CHUNK, W_GROUP)),
    ]
    out_specs = [pl.BlockSpec((tt, W_GROUP), lambda i: (i, 0)),
                 pl.BlockSpec((tt, W_GROUP), lambda i: (i, 0)),
                 pl.BlockSpec((tt, 3 * W_GROUP), lambda i: (i, 0))]
    out_shape = [jax.ShapeDtypeStruct((n, W_GROUP), F32),
                 jax.ShapeDtypeStruct((n, W_GROUP), F32),
                 jax.ShapeDtypeStruct((n, 3 * W_GROUP), BF16)]
    er = tt + 4 * HALO
    return pl.pallas_call(
        functools.partial(_mix_in_kernel, nct, tps),
        grid=(nt,),
        in_specs=in_specs, out_specs=out_specs, out_shape=out_shape,
        scratch_shapes=[pltpu.VMEM((er, PROJ_DIM), F32)] + [pltpu.VMEM((er, W_GROUP), F32)] * 4,
        compiler_params=pltpu.CompilerParams(vmem_limit_bytes=VMEM_LIMIT),
        name="mix_in",
    )(z, z, z, pos_tab, pos_tab, pos_tab, mods, lw["g_mix"], lw["w_in"], lw["conv_a_w"],
      lw["conv_a_b"], lw["conv_c_w"], lw["pool_w"], lw["pool_scale"], lw["sgu_g"],
      lw["sgu_w"], lw["sgu_b"])


def _chunk_scan(a, b, reverse):
    t = a.shape[0]
    row = lax.broadcasted_iota(jnp.int32, a.shape, 0)
    d = 1
    while d < t:
        shift = t - d if reverse else d
        ok = (row < t - d) if reverse else (row >= d)
        a_s = jnp.where(ok, pltpu.roll(a, shift, axis=0), 1.0)
        b_s = jnp.where(ok, pltpu.roll(b, shift, axis=0), 0.0)
        b = b + a * b_s
        a = a * a_s
        d *= 2
    return a, b


def _scan_kernel(xf_ref, xb_ref, wg_ref, bg_ref, lam_ref, hf_ref, hb_ref, carry_ref):
    tt = TOK_TILE
    s = pl.program_id(1)

    @pl.when(s == 0)
    def _():
        carry_ref[...] = jnp.zeros_like(carry_ref)

    for d, (x_ref, o_ref) in enumerate(((xf_ref, hf_ref), (xb_ref, hb_ref))):
        xa = x_ref[...]
        g = jnp.dot(xa.astype(BF16), wg_ref[d], preferred_element_type=F32) + bg_ref[d]
        r = jax.nn.sigmoid(g[:, 0:W_GROUP])
        gi = jax.nn.sigmoid(g[:, W_GROUP:2 * W_GROUP])
        neg_lam = -lam_ref[d:d + 1, :]
        softplus = jnp.maximum(neg_lam, 0.0) + jnp.log1p(jnp.exp(-jnp.abs(neg_lam)))
        log_a = (-LRU_C) * r * softplus
        a = jnp.exp(log_a)
        b = jnp.sqrt(1.0 - a * a) * (gi * xa)
        a_cum, h0 = _chunk_scan(a, b, reverse=(d == 1))
        h = h0 + a_cum * carry_ref[d:d + 1, :]
        o_ref[...] = h
        edge = 0 if d == 1 else tt - 1
        carry_ref[d:d + 1, :] = h[edge:edge + 1, :]


def _scan_call(xa, lw, nct, tps, n_batch):
    n, c = xa.shape
    tt = TOK_TILE
    fwd = lambda b, s: (jnp.where(s == 0, b, nct + b * tps + s - 1), 0)
    bwd = lambda b, s: (jnp.where(s == 0, b, nct + b * tps + tps - s), 0)
    full = lambda shape: pl.BlockSpec(shape, lambda b, s: (0,) * len(shape))
    return pl.pallas_call(
        _scan_kernel,
        grid=(n_batch, tps + 1),
        in_specs=[pl.BlockSpec((tt, c), fwd), pl.BlockSpec((tt, c), bwd),
                  full((2, c, 2 * c)), full((2, 1, 2 * c)), full((2, c))],
        out_specs=[pl.BlockSpec((tt, c), fwd), pl.BlockSpec((tt, c), bwd)],
        out_shape=[jax.ShapeDtypeStruct((n, c), F32)] * 2,
        scratch_shapes=[pltpu.VMEM((2, c), F32)],
        compiler_params=pltpu.CompilerParams(vmem_limit_bytes=VMEM_LIMIT),
        name="lru_scan",
    )(xa, xa, lw["lru_wg"], lw["lru_bg"], lw["lru_l"])


def _sort16_network():
    def merge(lo, hi, r):
        step = r * 2
        if step < hi - lo:
            yield from merge(lo, hi, step)
            yield from merge(lo + r, hi, step)
            yield from [(i, i + r) for i in range(lo + r, hi - r, step)]
        else:
            yield (lo, lo + r)

    def sort(lo, hi):
        if hi - lo >= 1:
            mid = lo + (hi - lo) // 2
            yield from sort(lo, mid)
            yield from sort(mid + 1, hi)
            yield from merge(lo, hi, 1)

    return tuple(sort(0, PEER_TOPK - 1))


_SORT16 = _sort16_network()


def _top16_of_columns(cols):
    cols = list(cols)
    rows = []
    for r in range(PEER_TOPK):
        mx = jnp.max(cols[0], axis=0, keepdims=True)
        rows.append(mx)
        if r + 1 < PEER_TOPK:
            hit = cols[0] == mx
            depth = PEER_TOPK - r
            for d in range(depth - 1):
                cols[d] = jnp.where(hit, cols[d + 1], cols[d])
            cols[depth - 1] = jnp.where(hit, NEG_BIG, cols[depth - 1])
    return rows


def _top16_of_keys(s):
    cols = [s[8 * g:8 * g + 8, :] for g in range(N_KEYS // 8)]
    for a, b in _SORT16:
        cols[a], cols[b] = jnp.maximum(cols[a], cols[b]), jnp.minimum(cols[a], cols[b])
    return _top16_of_columns(cols)


def _rank16(s, v):
    m8 = s >= v[7]
    m4 = s >= jnp.where(m8, v[3], v[11])
    m2 = s >= jnp.where(m8, jnp.where(m4, v[1], v[5]), jnp.where(m4, v[9], v[13]))
    m1 = s >= jnp.where(m8, jnp.where(m4, jnp.where(m2, v[0], v[2]), jnp.where(m2, v[4], v[6])),
                        jnp.where(m4, jnp.where(m2, v[8], v[10]), jnp.where(m2, v[12], v[14])))
    rank = (jnp.where(m8, 0.0, 8.0) + jnp.where(m4, 0.0, 4.0)
            + jnp.where(m2, 0.0, 2.0) + jnp.where(m1, 0.0, 1.0))
    return jnp.where(s >= v[15], rank, float(PEER_TOPK))


def _mid_kernel(z_ref, hf_ref, hb_ref, gag_ref, ybcd_ref, mod_ref, wout_ref, gffn_ref, wqt_ref,
                k1_ref, k2_ref, zo_ref, ft_ref, c1_ref, e1_ref, r2_ref, e2_ref, qt_scr):
    ya = ((hf_ref[...] + hb_ref[...]) * gag_ref[...]).astype(BF16)
    y = jnp.concatenate([ya, ybcd_ref[...]], axis=1)
    o = jnp.dot(y, wout_ref[...], preferred_element_type=F32)
    z = z_ref[...] + mod_ref[2:3, :] * o
    zo_ref[...] = z
    f = _rms(z, gffn_ref[...]) * (1.0 + mod_ref[4:5, :]) + mod_ref[3:4, :]
    ft = f.T.astype(BF16)
    ft_ref[...] = ft
    qt_scr[...] = jnp.dot(wqt_ref[...], ft, preferred_element_type=F32)
    lanes = 128

    def head_body(h, carry):
        base = pl.multiple_of(h * (2 * PEER_HALF), 2 * PEER_HALF)
        q1 = qt_scr[pl.ds(base, PEER_HALF), :].astype(BF16)
        q2 = qt_scr[pl.ds(base + PEER_HALF, PEER_HALF), :].astype(BF16)
        s1_all = jnp.dot(k1_ref[...], q1, preferred_element_type=F32)
        s2_all = jnp.dot(k2_ref[...], q2, preferred_element_type=F32)
        for lt in range(MID_TILE // lanes):
            sl = slice(lt * lanes, (lt + 1) * lanes)
            s1 = s1_all[:, sl]
            s2 = s2_all[:, sl]
            v1r = _top16_of_keys(s1)
            v2r = _top16_of_keys(s2)
            r2 = _rank16(s2, v2r)
            v1 = jnp.concatenate(v1r, axis=0)
            v2 = jnp.concatenate(v2r, axis=0)
            tops = _top16_of_columns([v1 + v2r[b] for b in range(PEER_TOPK)])
            m = tops[0]
            tau = tops[PEER_TOPK - 1]
            zsum = jnp.zeros_like(m)
            for row in tops:
                zsum = zsum + jnp.exp(row - m)
            theta = jnp.full(v2.shape, -NEG_BIG, F32)
            for a in range(PEER_TOPK):
                theta = jnp.minimum(theta, jnp.where(v1r[a] + v2 >= tau, v1r[a], -NEG_BIG))
            th = [theta[b:b + 1, :] for b in range(8)]
            m4 = s1 >= th[3]
            m2 = s1 >= jnp.where(m4, th[5], th[1])
            m1 = s1 >= jnp.where(m4, jnp.where(m2, th[6], th[4]), jnp.where(m2, th[2], th[0]))
            c1 = jnp.where(m4, 4.0, 0.0) + jnp.where(m2, 2.0, 0.0) + jnp.where(m1, 1.0, 0.0)
            c1 = jnp.where(s1 >= th[7], 8.0, c1)
            n_hi = jnp.sum(jnp.where(theta[8:16, :] < 1.0e38, 1.0, 0.0), axis=0, keepdims=True)
            c1 = c1 + jnp.where(s1 >= v1r[0], n_hi, 0.0)
            c1_ref[h, :, sl] = c1
            e1_ref[h, :, sl] = jnp.exp(s1 - v1r[0]) * (1.0 / zsum)
            r2_ref[h, :, sl] = r2.astype(BF16)
            e2_ref[h, :, sl] = jnp.exp(s2 - v2r[0]).astype(BF16)
        return carry

    def head_group(hg, carry):
        for j in range(MID_HEADS_PER_ITER):
            head_body(MID_HEADS_PER_ITER * hg + j, carry)
        return carry

    lax.fori_loop(0, PEER_HEADS // MID_HEADS_PER_ITER, head_group, 0)


def _mid_call(z, hf, hb, gag, ybcd, mods, lw, layer, n_ctx_tok, seq, n_batch, skip_ctx):
    n, d = z.shape
    tt = MID_TILE
    nct = n_ctx_tok // tt
    tps = seq // tt
    tile_off = nct if skip_ctx else 0
    nt = n // tt - tile_off
    nq = PEER_HEADS * 2 * PEER_HALF

    def mrow(i):
        j = i + tile_off
        return jnp.where(j < nct, n_batch, (j - nct) // tps)

    full = lambda shape: pl.BlockSpec(shape, lambda i: (0,) * len(shape))
    tok = lambda w: pl.BlockSpec((tt, w), lambda i: (i + tile_off, 0))
    in_specs = [tok(d), tok(W_GROUP), tok(W_GROUP), tok(W_GROUP), tok(3 * W_GROUP),
                pl.BlockSpec((None, N_MOD, d), lambda i: (mrow(i), 0, 0)),
                pl.BlockSpec((None, d, d), lambda i: (layer, 0, 0)), full((1, d)),
                pl.BlockSpec((None, nq, d), lambda i: (layer, 0, 0)),
                full((N_KEYS, PEER_HALF)), full((N_KEYS, PEER_HALF))]
    hk = lambda: pl.BlockSpec((PEER_HEADS, N_KEYS, tt), lambda i: (0, 0, i + tile_off))
    out_specs = [tok(d), pl.BlockSpec((d, tt), lambda i: (0, i + tile_off)), hk(), hk(), hk(), hk()]
    hk_shape = lambda dt: jax.ShapeDtypeStruct((PEER_HEADS, N_KEYS, n), dt)
    out_shape = [jax.ShapeDtypeStruct((n, d), F32), jax.ShapeDtypeStruct((d, n), BF16),
                 hk_shape(F32), hk_shape(F32), hk_shape(BF16), hk_shape(BF16)]
    return pl.pallas_call(
        _mid_kernel,
        grid=(nt,),
        in_specs=in_specs, out_specs=out_specs, out_shape=out_shape,
        scratch_shapes=[pltpu.VMEM((nq, tt), F32)],
        compiler_params=pltpu.CompilerParams(vmem_limit_bytes=VMEM_LIMIT),
        name="mid",
    )(z, hf, hb, gag, ybcd, mods, lw["w_out"], lw["g_ffn"], lw["wq_t"], lw["k1"], lw["k2"])


def _peer_kernel(final, layer, ft_ref, c1_ref, e1_ref, r2_ref, e2_ref, u_hbm, vt_hbm, z_ref, mod_ref,
                 gfin_ref, zo_ref, acc_ref, a_scr, h_scr0, h_scr1, w_scr0, w_scr1,
                 u_buf, vt_buf, sems):
    c = pl.program_id(1)
    nc = pl.num_programs(1)
    tl = PEER_LANES
    sub = PEER_SUB
    ec = PEER_ECHUNK
    pk = BF16_ROWS
    n_sub = ec // sub
    h_scr = (h_scr0, h_scr1)
    w_scr = (w_scr0, w_scr1)
    step = pl.program_id(0) * nc + c
    n_steps = pl.num_programs(0) * nc
    ahead = PEER_BUFS - 1

    def chunk_copies(s):
        slot = s % PEER_BUFS
        first = pl.multiple_of((s % nc) * ec, ec)
        return (pltpu.make_async_copy(u_hbm.at[layer, pl.ds(first, ec), :], u_buf.at[slot],
                                      sems.at[0, slot]),
                pltpu.make_async_copy(vt_hbm.at[layer, :, pl.ds(first, ec)], vt_buf.at[slot],
                                      sems.at[1, slot]))

    @pl.when(step == 0)
    def _():
        for s in range(ahead):
            for cp in chunk_copies(s):
                cp.start()

    @pl.when(step + ahead < n_steps)
    def _():
        for cp in chunk_copies(step + ahead):
            cp.start()

    for cp in chunk_copies(step):
        cp.wait()
    u_ref = u_buf.at[step % PEER_BUFS]
    vt_ref = vt_buf.at[step % PEER_BUFS]

    @pl.when(c == 0)
    def _():
        acc_ref[...] = jnp.zeros_like(acc_ref)

    def pre_act(k):
        h_scr[(k // 2) % 2][...] = jnp.dot(u_ref[k * sub:(k + 2) * sub, :], ft_ref[...],
                                           preferred_element_type=F32)

    def gates(k):
        for il in range(sub // N_KEYS):
            i_loc = k * (sub // N_KEYS) + il
            for lt in range(PEER_TILE // tl):
                ls = slice(lt * tl, (lt + 1) * tl)
                w = jnp.zeros((N_KEYS // pk, pk, tl), BF16)
                for h in range(PEER_HEADS):
                    c1 = jnp.broadcast_to(c1_ref[h, i_loc:i_loc + 1, ls], (pk, tl)).astype(BF16)
                    e1 = jnp.broadcast_to(e1_ref[h, i_loc:i_loc + 1, ls], (pk, tl)).astype(BF16)
                    sel = jnp.where(r2_ref[h, :, :, ls] < c1[None], e2_ref[h, :, :, ls], 0.0)
                    w = w + sel * e1[None]
                w_scr[k % 2][il * N_KEYS:(il + 1) * N_KEYS, ls] = w.reshape(N_KEYS, tl)

    def activate(k):
        half = k % 2
        g = _gelu_sigmoid(h_scr[(k // 2) % 2][half * sub:(half + 1) * sub, :].astype(BF16))
        a_scr[k * sub:(k + 1) * sub, :] = g * w_scr[k % 2][...]

    def project(k):
        acc_ref[...] += jnp.dot(vt_ref[:, k * sub:(k + 1) * sub], a_scr[k * sub:(k + 1) * sub, :],
                                preferred_element_type=F32)

    pre_act(0)
    gates(0)
    for k in range(n_sub):
        if k % 2 == 0 and k + 2 < n_sub:
            pre_act(k + 2)
        if k >= 1:
            project(k - 1)
        activate(k)
        if k + 1 < n_sub:
            gates(k + 1)
    project(n_sub - 1)

    @pl.when(c == pl.num_programs(1) - 1)
    def _():
        z = z_ref[...] + mod_ref[5:6, :] * acc_ref[...].T
        zo_ref[...] = _rms(z, gfin_ref[...]) if final else z


def _peer_call(z, ft, c1, e1, r2, e2, mods, lw, layer, g_final, n_ctx_tok, seq, n_batch, final):
    n, d = z.shape
    t = PEER_TILE
    ec = PEER_ECHUNK
    n_exp = lw["u"].shape[1]
    assert n_exp % ec == 0 and (ec // PEER_SUB) % 2 == 0
    nct = n_ctx_tok // t
    tps = seq // t
    ic = ec // N_KEYS
    pk = BF16_ROWS
    off = nct if final else 0
    r2 = r2.reshape(PEER_HEADS, N_KEYS // pk, pk, n)
    e2 = e2.reshape(PEER_HEADS, N_KEYS // pk, pk, n)

    def mrow(i):
        j = i + off
        return jnp.where(j < nct, n_batch, (j - nct) // tps)

    in_specs = [
        pl.BlockSpec((d, t), lambda i, c: (0, i + off)),
        pl.BlockSpec((PEER_HEADS, ic, t), lambda i, c: (0, c, i + off)),
        pl.BlockSpec((PEER_HEADS, ic, t), lambda i, c: (0, c, i + off)),
        pl.BlockSpec((PEER_HEADS, N_KEYS // pk, pk, t), lambda i, c: (0, 0, 0, i + off)),
        pl.BlockSpec((PEER_HEADS, N_KEYS // pk, pk, t), lambda i, c: (0, 0, 0, i + off)),
        pl.BlockSpec(memory_space=pl.ANY),
        pl.BlockSpec(memory_space=pl.ANY),
        pl.BlockSpec((t, d), lambda i, c: (i + off, 0)),
        pl.BlockSpec((None, N_MOD, d), lambda i, c: (mrow(i), 0, 0)),
        pl.BlockSpec((1, d), lambda i, c: (0, 0)),
    ]
    return pl.pallas_call(
        functools.partial(_peer_kernel, final, layer),
        grid=(n // t - off, n_exp // ec),
        in_specs=in_specs,
        out_specs=pl.BlockSpec((t, d), lambda i, c: (i, 0)),
        out_shape=jax.ShapeDtypeStruct((n - off * t, d), F32),
        scratch_shapes=[pltpu.VMEM((d, t), F32), pltpu.VMEM((ec, t), BF16)]
        + [pltpu.VMEM((2 * PEER_SUB, t), F32)] * 2 + [pltpu.VMEM((PEER_SUB, t), BF16)] * 2
        + [pltpu.VMEM((PEER_BUFS, ec, d), BF16), pltpu.VMEM((PEER_BUFS, d, ec), BF16),
           pltpu.SemaphoreType.DMA((2, PEER_BUFS))],
        compiler_params=pltpu.CompilerParams(vmem_limit_bytes=VMEM_LIMIT),
        name="peer",
    )(ft, c1, e1, r2, e2, lw["u"], lw["v_t"], z, mods, g_final)


def _block_diag(w):
    h, a, b = w.shape
    eye = jnp.eye(h, dtype=w.dtype)
    return (eye[:, None, :, None] * w[:, :, None, :]).reshape(h * a, h * b)


def _grid_sincos(n, dim):
    rows = n // GRID_W
    quarter = dim // 4
    omega = POS_BASE ** (-jnp.arange(quarter, dtype=F32) / quarter)

    def axis_emb(p):
        ang = p[:, None] * omega[None, :]
        return jnp.concatenate([jnp.sin(ang), jnp.cos(ang)], axis=-1)

    row_emb = jnp.repeat(axis_emb(jnp.arange(rows, dtype=F32)), GRID_W, axis=0)
    col_emb = jnp.tile(axis_emb(jnp.arange(GRID_W, dtype=F32)), (rows, 1))
    return jnp.concatenate([row_emb, col_emb], axis=-1)


def kernel(x, c, ctx, c_ctx, w_ada, b_ada, g_mix, g_ffn, w_in, w_out, conv_a_w, conv_a_b, lru_l, lru_wr, lru_br, lru_wi, lru_bi, pool_w, pool_scale, conv_c_w, sgu_g, sgu_w, sgu_b, peer_wq, peer_k1, peer_k2, peer_u, peer_v, g_final):
    bsz, seq, d = x.shape
    ctx_len = ctx.shape[1]
    depth = w_ada.shape[0]
    assert ctx_len == TOK_TILE and seq % TOK_TILE == 0 and seq % PEER_TILE == 0 and seq % MID_TILE == 0
    assert (bsz * ctx_len) % PEER_TILE == 0 and (bsz * ctx_len) % MID_TILE == 0 and bsz + 1 <= 8
    n_ctx_tok = bsz * ctx_len
    nct = n_ctx_tok // TOK_TILE
    tps = seq // TOK_TILE

    cvec = jnp.concatenate([c, c_ctx[None, :], jnp.zeros((8 - bsz - 1, d), F32)], axis=0)
    mods_all = _ada_call(cvec, w_ada, b_ada).reshape(depth, 8, N_MOD, d)

    pos_tab = jnp.concatenate([jnp.zeros((TOK_TILE, d), F32), _grid_sincos(seq, d)], axis=0)
    z = jnp.concatenate([ctx.reshape(n_ctx_tok, d), x.reshape(bsz * seq, d)], axis=0)

    stacked = {
        "w_in": w_in.astype(BF16), "w_out": w_out.astype(BF16),
        "wq_t": jnp.swapaxes(peer_wq, 1, 2).astype(BF16),
        "u": peer_u.astype(BF16), "v_t": jnp.swapaxes(peer_v, 1, 2).astype(BF16),
    }

    for l in range(depth):
        wg = jnp.stack([jnp.concatenate([_block_diag(lru_wr[l, dd]), _block_diag(lru_wi[l, dd])], axis=1)
                        for dd in range(2)]).astype(BF16)
        bg = jnp.stack([jnp.concatenate([lru_br[l, dd].reshape(1, -1), lru_bi[l, dd].reshape(1, -1)], axis=1)
                        for dd in range(2)])
        lw = {
            "g_mix": g_mix[l].reshape(1, d), "g_ffn": g_ffn[l].reshape(1, d),
            "conv_a_w": conv_a_w[l], "conv_a_b": conv_a_b[l].reshape(1, -1),
            "conv_c_w": conv_c_w[l],
            "pool_w": _block_diag(pool_w[l]).astype(BF16), "pool_scale": pool_scale[l].reshape(1, -1),
            "sgu_g": sgu_g[l].reshape(1, -1),
            "sgu_w": jnp.transpose(sgu_w[l], (1, 0, 2)).reshape(SGU_CHUNK, SGU_HEADS * SGU_CHUNK).astype(BF16),
            "sgu_b": jnp.repeat(sgu_b[l].T, W_GROUP // SGU_HEADS, axis=1),
            "lru_wg": wg, "lru_bg": bg, "lru_l": lru_l[l],
            "k1": peer_k1[l].astype(BF16), "k2": peer_k2[l].astype(BF16),
        }
        lw.update(stacked)
        mods = mods_all[l]
        xa, gag, ybcd = _mix_in_call(z, pos_tab, mods, lw, l, nct, tps, bsz)
        hf, hb = _scan_call(xa, lw, nct, tps, bsz)
        last = l == depth - 1
        z, ft, c1, e1, r2, e2 = _mid_call(z, hf, hb, gag, ybcd, mods, lw, l, n_ctx_tok, seq, bsz, last)
        z = _peer_call(z, ft, c1, e1, r2, e2, mods, lw, l, g_final.reshape(1, d), n_ctx_tok, seq,
                       bsz, last)

    return z.reshape(bsz, seq, d)
```
